```python
import math
import jax, jax.numpy as jnp
from jax import lax
import numpy as np

D_MODEL = 2048
BATCH = 1
SEQ = 8192
DEPTH = 4
DEC_BATCH = 4
DEC_SEQ = 4096
PAST_LEN = 128

GRID_W = 64
Q_BLOCK = 128
ROPE_THETA = 10000.0
EPS = 1e-5
HEAD_DIM = 128
A_HEADS = 6
A_KV_HEADS = 2
A_GROUP = A_HEADS // A_KV_HEADS
B_HEADS = 6
B_Q_RANK = 384
B_KV_RANK = 256
B_NOPE = 128
B_ROPE = 64
B_V = 128
C_HEADS = 4
C_QK = 64
C_V = 2 * C_QK
MIX_WIDTH = A_HEADS * HEAD_DIM + B_HEADS * B_V + C_HEADS * C_V
IN_WIDTHS = (A_HEADS * HEAD_DIM, A_KV_HEADS * HEAD_DIM, A_KV_HEADS * HEAD_DIM,
             B_Q_RANK, B_KV_RANK, B_ROPE,
             C_HEADS * 2 * C_QK, C_HEADS * 2 * C_QK, C_HEADS * C_V)
IN_WIDTH = sum(IN_WIDTHS)
PEER_HEADS = 8
PEER_NKEYS = 128
PEER_EXPERTS = PEER_NKEYS * PEER_NKEYS
PEER_TOPK = 16
PEER_QDIM = 256
PEER_TOK_BLOCK = 128
DN_ALPHA = (2 * DEPTH) ** 0.25
DN_BETA = (8 * DEPTH) ** -0.25

kernel_name = 'hybrid_gqa_mla_diff_peer_encoder'


def rms_norm(x, g):
    xf = x.astype(jnp.float32)
    y = xf * lax.rsqrt(jnp.mean(xf * xf, axis=-1, keepdims=True) + EPS)
    return (y * g.astype(jnp.float32)).astype(x.dtype)


def layer_norm(x, g, b):
    xf = x.astype(jnp.float32)
    mu = jnp.mean(xf, axis=-1, keepdims=True)
    xc = xf - mu
    var = jnp.mean(xc * xc, axis=-1, keepdims=True)
    y = xc * lax.rsqrt(var + EPS) * g.astype(jnp.float32) + b.astype(jnp.float32)
    return y.astype(x.dtype)


def rope(x, pos):
    d = x.shape[-1]
    half = d // 2
    inv = jnp.power(ROPE_THETA, -jnp.arange(half, dtype=jnp.float32) / half)
    ang = pos.astype(jnp.float32)[:, None] * inv[None, :]
    shape = (1, pos.shape[0]) + (1,) * (x.ndim - 3) + (half,)
    cos = jnp.cos(ang).reshape(shape)
    sin = jnp.sin(ang).reshape(shape)
    xf = x.astype(jnp.float32)
    x1, x2 = xf[..., :half], xf[..., half:]
    return jnp.concatenate([x1 * cos - x2 * sin, x2 * cos + x1 * sin], axis=-1).astype(x.dtype)


def rope_axial(x, rows, cols):
    half = x.shape[-1] // 2
    return jnp.concatenate([rope(x[..., :half], rows), rope(x[..., half:], cols)], axis=-1)


def sweep_query_blocks(fn, qs):
    b, s = qs[0].shape[:2]
    nb = s // Q_BLOCK
    blocks = tuple(jnp.moveaxis(q.reshape((b, nb, Q_BLOCK) + q.shape[2:]), 1, 0) for q in qs)
    out = lax.map(fn, blocks)
    return jnp.moveaxis(out, 0, 1).reshape((b, s) + out.shape[3:])


def split_columns(h):
    out, start = [], 0
    for w in IN_WIDTHS:
        out.append(h[..., start:start + w])
        start += w
    return out


def mixer_gqa(q, k, v, qn_g, kn_g, rows, cols):
    b, s, _ = q.shape
    q = rope_axial(rms_norm(q.reshape(b, s, A_HEADS, HEAD_DIM), qn_g), rows, cols)
    k = rope_axial(rms_norm(k.reshape(b, s, A_KV_HEADS, HEAD_DIM), kn_g), rows, cols)
    v = v.reshape(b, s, A_KV_HEADS, HEAD_DIM)
    q = q.reshape(b, s, A_KV_HEADS, A_GROUP, HEAD_DIM)
    scale = HEAD_DIM ** -0.5

    def block(args):
        (qb,) = args
        sc = jnp.einsum('bqhgd,bkhd->bhgqk', qb, k).astype(jnp.float32) * scale
        p = jax.nn.softmax(sc, axis=-1).astype(v.dtype)
        return jnp.einsum('bhgqk,bkhd->bqhgd', p, v)

    o = sweep_query_blocks(block, (q,))
    return o.reshape(b, s, A_HEADS * HEAD_DIM)


def mixer_mla(c_q, c_kv, k_rope, qa_g, kva_g, w_uq, w_ukv, pos):
    b, s, _ = c_q.shape
    qh = (rms_norm(c_q, qa_g) @ w_uq).reshape(b, s, B_HEADS, B_NOPE + B_ROPE)
    q_nope = qh[..., :B_NOPE]
    q_pe = rope(qh[..., B_NOPE:], pos)
    kvh = (rms_norm(c_kv, kva_g) @ w_ukv).reshape(b, s, B_HEADS, B_NOPE + B_V)
    k_nope = kvh[..., :B_NOPE]
    v = kvh[..., B_NOPE:]
    k_pe = rope(k_rope, pos)
    scale = (B_NOPE + B_ROPE) ** -0.5

    def block(args):
        qn, qp = args
        sc = (jnp.einsum('bqhd,bkhd->bhqk', qn, k_nope)
              + jnp.einsum('bqhr,bkr->bhqk', qp, k_pe)).astype(jnp.float32) * scale
        p = jax.nn.softmax(sc, axis=-1).astype(v.dtype)
        return jnp.einsum('bhqk,bkhd->bqhd', p, v)

    o = sweep_query_blocks(block, (q_nope, q_pe))
    return o.reshape(b, s, B_HEADS * B_V)


def mixer_diff(q, k, v, lq1, lk1, lq2, lk2, subln_g, lambda_init, pos):
    b, s, _ = q.shape
    q = rope(q.reshape(b, s, C_HEADS, 2, C_QK), pos)
    k = rope(k.reshape(b, s, C_HEADS, 2, C_QK), pos)
    v = v.reshape(b, s, C_HEADS, C_V)
    f32 = jnp.float32
    lam = (jnp.exp(jnp.sum(lq1.astype(f32) * lk1.astype(f32)))
           - jnp.exp(jnp.sum(lq2.astype(f32) * lk2.astype(f32))) + lambda_init)
    scale = C_QK ** -0.5

    def block(args):
        (qb,) = args
        sc = jnp.einsum('bqhmd,bkhmd->bhmqk', qb, k).astype(f32) * scale
        p = jax.nn.softmax(sc, axis=-1)
        w = (p[:, :, 0] - lam * p[:, :, 1]).astype(v.dtype)
        return jnp.einsum('bhqk,bkhd->bqhd', w, v)

    o = sweep_query_blocks(block, (q,))
    o = rms_norm(o, subln_g) * (1.0 - lambda_init)
    return o.reshape(b, s, C_HEADS * C_V)


def peer_ffn(x, w_query, sub_keys, expert_u, expert_v):
    b, s, d = x.shape
    xt = x.reshape(-1, PEER_TOK_BLOCK, d)
    half = PEER_QDIM // 2

    def block(xb):
        t = xb.shape[0]
        q = (xb @ w_query).reshape(t, PEER_HEADS, 2, half)
        s_sub = jnp.einsum('thmc,mnc->thmn', q, sub_keys).astype(jnp.float32)
        v1, i1 = lax.top_k(s_sub[:, :, 0], PEER_TOPK)
        v2, i2 = lax.top_k(s_sub[:, :, 1], PEER_TOPK)
        cand = (v1[..., :, None] + v2[..., None, :]).reshape(t, PEER_HEADS, PEER_TOPK * PEER_TOPK)
        cand_id = (i1[..., :, None] * PEER_NKEYS + i2[..., None, :]).reshape(t, PEER_HEADS, PEER_TOPK * PEER_TOPK)
        top_s, top_pos = lax.top_k(cand, PEER_TOPK)
        idx = jnp.take_along_axis(cand_id, top_pos, axis=-1)
        g = jax.nn.softmax(top_s, axis=-1)
        u = expert_u[idx]
        act = jax.nn.gelu(jnp.einsum('td,thkd->thk', xb, u).astype(jnp.float32), approximate=False)
        w = (g * act).astype(xb.dtype)
        return jnp.einsum('thk,thkd->td', w, expert_v[idx])

    y = lax.map(block, xt)
    return y.reshape(b, s, d)


def encoder_layer(x, pos, rows, cols, lambda_init, w_in, a_qnorm_g, a_knorm_g, b_qnorm_g, b_kvnorm_g,
                  b_w_uq, b_w_ukv, c_lambda_q1, c_lambda_k1, c_lambda_q2, c_lambda_k2, c_subln_g,
                  w_o, ln1_g, ln1_b, peer_w_query, peer_sub_keys, peer_u, peer_v, ln2_g, ln2_b):
    a_q, a_k, a_v, b_cq, b_ckv, b_kr, c_q, c_k, c_v = split_columns(x @ w_in)
    out_a = mixer_gqa(a_q, a_k, a_v, a_qnorm_g, a_knorm_g, rows, cols)
    out_b = mixer_mla(b_cq, b_ckv, b_kr, b_qnorm_g, b_kvnorm_g, b_w_uq, b_w_ukv, pos)
    out_c = mixer_diff(c_q, c_k, c_v, c_lambda_q1, c_lambda_k1, c_lambda_q2, c_lambda_k2,
                       c_subln_g, lambda_init, pos)
    mix = jnp.concatenate([out_a, out_b, out_c], axis=-1) @ w_o
    x = layer_norm(DN_ALPHA * x + mix, ln1_g, ln1_b)
    x = layer_norm(DN_ALPHA * x + peer_ffn(x, peer_w_query, peer_sub_keys, peer_u, peer_v), ln2_g, ln2_b)
    return x


def trunk(x, w_in, a_qnorm_g, a_knorm_g, b_qnorm_g, b_kvnorm_g, b_w_uq, b_w_ukv,
          c_lambda_q1, c_lambda_k1, c_lambda_q2, c_lambda_k2, c_subln_g, w_o, ln1_g, ln1_b,
          peer_w_query, peer_sub_keys, peer_u, peer_v, ln2_g, ln2_b):
    s = x.shape[1]
    n_rows = s // GRID_W
    pos = jnp.arange(s, dtype=jnp.int32)
    rows = jnp.repeat(jnp.arange(n_rows, dtype=jnp.int32), GRID_W)
    cols = jnp.broadcast_to(jnp.arange(GRID_W, dtype=jnp.int32)[None, :], (n_rows, GRID_W)).reshape(-1)
    for l in range(DEPTH):
        lambda_init = 0.8 - 0.6 * math.exp(-0.3 * l)
        x = encoder_layer(x, pos, rows, cols, lambda_init, w_in[l], a_qnorm_g[l], a_knorm_g[l],
                          b_qnorm_g[l], b_kvnorm_g[l], b_w_uq[l], b_w_ukv[l],
                          c_lambda_q1[l], c_lambda_k1[l], c_lambda_q2[l], c_lambda_k2[l], c_subln_g[l],
                          w_o[l], ln1_g[l], ln1_b[l], peer_w_query[l], peer_sub_keys[l],
                          peer_u[l], peer_v[l], ln2_g[l], ln2_b[l])
    return x


def setup_inputs(seed: int = 0) -> dict:
    key = jax.random.key(seed)
    ks = jax.random.split(key, 24)
    f32 = jnp.float32
    L = DEPTH

    def nrm(k, shape, scale):
        return jax.random.normal(k, shape, f32) * scale

    def gain(k, shape):
        return 1.0 + 0.02 * jax.random.normal(k, shape, f32)

    return {
        'x_prompt': nrm(ks[0], (BATCH, SEQ, D_MODEL), 1.0),
        'x_sample': nrm(ks[1], (DEC_BATCH, DEC_SEQ, D_MODEL), 1.0),
        'w_in': nrm(ks[2], (L, D_MODEL, IN_WIDTH), D_MODEL ** -0.5),
        'a_qnorm_g': gain(ks[3], (L, HEAD_DIM)),
        'a_knorm_g': gain(ks[4], (L, HEAD_DIM)),
        'b_qnorm_g': gain(ks[5], (L, B_Q_RANK)),
        'b_kvnorm_g': gain(ks[6], (L, B_KV_RANK)),
        'b_w_uq': nrm(ks[7], (L, B_Q_RANK, B_HEADS * (B_NOPE + B_ROPE)), B_Q_RANK ** -0.5),
        'b_w_ukv': nrm(ks[8], (L, B_KV_RANK, B_HEADS * (B_NOPE + B_V)), B_KV_RANK ** -0.5),
        'c_lambda_q1': nrm(ks[9], (L, C_QK), 0.1),
        'c_lambda_k1': nrm(ks[10], (L, C_QK), 0.1),
        'c_lambda_q2': nrm(ks[11], (L, C_QK), 0.1),
        'c_lambda_k2': nrm(ks[12], (L, C_QK), 0.1),
        'c_subln_g': gain(ks[13], (L, C_V)),
        'w_o': nrm(ks[14], (L, MIX_WIDTH, D_MODEL), DN_BETA * MIX_WIDTH ** -0.5),
        'ln1_g': gain(ks[15], (L, D_MODEL)),
        'ln1_b': nrm(ks[16], (L, D_MODEL), 0.02),
        'peer_w_query': nrm(ks[17], (L, D_MODEL, PEER_HEADS * PEER_QDIM), D_MODEL ** -0.5),
        'peer_sub_keys': nrm(ks[18], (L, 2, PEER_NKEYS, PEER_QDIM // 2), (PEER_QDIM // 2) ** -0.5),
        'peer_u': nrm(ks[19], (L, PEER_EXPERTS, D_MODEL), D_MODEL ** -0.5),
        'peer_v': nrm(ks[20], (L, PEER_EXPERTS, D_MODEL), DN_BETA * PEER_HEADS ** -0.5),
        'ln2_g': gain(ks[21], (L, D_MODEL)),
        'ln2_b': nrm(ks[22], (L, D_MODEL), 0.02),
    }


def reference(x_prompt, x_sample, w_in, a_qnorm_g, a_knorm_g, b_qnorm_g, b_kvnorm_g, b_w_uq, b_w_ukv,
              c_lambda_q1, c_lambda_k1, c_lambda_q2, c_lambda_k2, c_subln_g, w_o, ln1_g, ln1_b,
              peer_w_query, peer_sub_keys, peer_u, peer_v, ln2_g, ln2_b):
    y_prompt = trunk(x_prompt, w_in, a_qnorm_g, a_knorm_g, b_qnorm_g, b_kvnorm_g, b_w_uq, b_w_ukv,
                     c_lambda_q1, c_lambda_k1, c_lambda_q2, c_lambda_k2, c_subln_g, w_o, ln1_g, ln1_b,
                     peer_w_query, peer_sub_keys, peer_u, peer_v, ln2_g, ln2_b)
    y_sample = trunk(x_sample, w_in, a_qnorm_g, a_knorm_g, b_qnorm_g, b_kvnorm_g, b_w_uq, b_w_ukv,
                     c_lambda_q1, c_lambda_k1, c_lambda_q2, c_lambda_k2, c_subln_g, w_o, ln1_g, ln1_b,
                     peer_w_query, peer_sub_keys, peer_u, peer_v, ln2_g, ln2_b)
    return (y_prompt, y_sample)
```

```python
import functools
import math

import jax
import jax.numpy as jnp
from jax import lax
from jax.experimental import pallas as pl
from jax.experimental.pallas import tpu as pltpu

D_MODEL = 2048
DEPTH = 4
GRID_W = 64
ROPE_THETA = 10000.0
EPS = 1e-5
HEAD_DIM = 128
A_HEADS, A_KV_HEADS = 6, 2
B_HEADS, B_Q_RANK, B_KV_RANK, B_NOPE, B_ROPE, B_V = 6, 384, 256, 128, 64, 128
C_HEADS, C_QK = 4, 64
C_V = 2 * C_QK
PEER_HEADS, PEER_NKEYS, PEER_TOPK, PEER_QDIM = 8, 128, 16, 256
PEER_SLOTS = PEER_HEADS * PEER_TOPK
DN_ALPHA = (2 * DEPTH) ** 0.25

LANES = 128
SUBLANES = 8
VMEM_LIMIT = 56 * 1024 * 1024

TOK_TILE = 256
ATT_TQ = 512
ATT_TK = 512
PEER_TILE = 128
PEER_GROUP = SUBLANES

_A_Q, _A_K, _A_V = 0, 768, 1024
_B_CQ, _B_CKV, _B_KR = 1280, 1664, 1920
_C_Q, _C_K, _C_V_OFF = 2048, 2560, 3072
IN_WIDTH_PAD = 3584

_NT = (((1,), (1,)), ((), ()))


def _cparams(sem):
    return pltpu.CompilerParams(dimension_semantics=sem, vmem_limit_bytes=VMEM_LIMIT)


def _full(shape):
    return pl.BlockSpec(shape, lambda *_: (0,) * len(shape))


def _rms(x, g):
    return x * lax.rsqrt(jnp.mean(x * x, axis=-1, keepdims=True) + EPS) * g


def _layer_norm(y, g, b):
    mu = jnp.mean(y, axis=-1, keepdims=True)
    yc = y - mu
    var = jnp.mean(yc * yc, axis=-1, keepdims=True)
    return yc * lax.rsqrt(var + EPS) * g + b


def _rope64(x, cos, sin_signed):
    lane = lax.broadcasted_iota(jnp.int32, x.shape, 1)
    first = (lane % 64) < 32
    partner = jnp.where(first, pltpu.roll(x, 96, 1), pltpu.roll(x, 32, 1))
    return x * cos + partner * sin_signed


def _inproj_kernel(x_ref, w_ref, cosa_ref, sina_ref, cosp_ref, sinp_ref,
                   aq_g_ref, ak_g_ref, bq_g_ref, bkv_g_ref, wuq_ref, wuk_ref, wuv_ref,
                   qa_ref, ka_ref, va_ref, qb_ref, kb_ref, vb_ref, qc_ref, kc_ref, vc_ref):
    bf = jnp.bfloat16
    xb = x_ref[...].astype(bf)

    def proj(lo, hi):
        return jnp.dot(xb, w_ref[:, lo:hi], preferred_element_type=jnp.float32)

    cosa, sina = cosa_ref[...], sina_ref[...]
    cosp, sinp = cosp_ref[...], sinp_ref[...]

    hq = proj(_A_Q, _A_K)
    for h in range(A_HEADS):
        sl = slice(h * HEAD_DIM, (h + 1) * HEAD_DIM)
        qa_ref[:, sl] = _rope64(_rms(hq[:, sl], aq_g_ref[...]), cosa, sina).astype(bf)
    hk = proj(_A_K, _A_V)
    for h in range(A_KV_HEADS):
        sl = slice(h * HEAD_DIM, (h + 1) * HEAD_DIM)
        ka_ref[:, sl] = _rope64(_rms(hk[:, sl], ak_g_ref[...]), cosa, sina).astype(bf)
    va_ref[...] = proj(_A_V, _B_CQ).astype(bf)

    cq = _rms(proj(_B_CQ, _B_CKV), bq_g_ref[...]).astype(bf)
    qh = jnp.dot(cq, wuq_ref[...], preferred_element_type=jnp.float32)
    ckv = _rms(proj(_B_CKV, _B_KR), bkv_g_ref[...]).astype(bf)
    k_nope = jnp.dot(ckv, wuk_ref[...], preferred_element_type=jnp.float32)
    vb_ref[...] = jnp.dot(ckv, wuv_ref[...], preferred_element_type=jnp.float32).astype(bf)
    k_pe = _rope64(proj(_B_KR, _C_Q), cosp, sinp).astype(bf)
    for h in range(B_HEADS):
        lo = h * 2 * LANES
        qb_ref[:, lo:lo + LANES] = qh[:, lo:lo + LANES].astype(bf)
        qb_ref[:, lo + LANES:lo + 2 * LANES] = _rope64(qh[:, lo + LANES:lo + 2 * LANES], cosp, sinp).astype(bf)
        kb_ref[:, lo:lo + LANES] = k_nope[:, h * LANES:(h + 1) * LANES].astype(bf)
        kb_ref[:, lo + LANES:lo + 2 * LANES] = k_pe

    hq = proj(_C_Q, _C_K)
    lane = lax.broadcasted_iota(jnp.int32, (x_ref.shape[0], LANES), 1)
    for h in range(C_HEADS):
        sl = slice(h * LANES, (h + 1) * LANES)
        qr = _rope64(hq[:, sl], cosp, sinp)
        qc_ref[:, (2 * h) * LANES:(2 * h + 1) * LANES] = jnp.where(lane < C_QK, qr, 0.0).astype(bf)
        qc_ref[:, (2 * h + 1) * LANES:(2 * h + 2) * LANES] = jnp.where(lane < C_QK, 0.0, qr).astype(bf)
    hk = proj(_C_K, _C_V_OFF)
    for h in range(C_HEADS):
        sl = slice(h * LANES, (h + 1) * LANES)
        kc_ref[:, sl] = _rope64(hk[:, sl], cosp, sinp).astype(bf)
    vc_ref[...] = proj(_C_V_OFF, IN_WIDTH_PAD).astype(bf)


def _inproj(x, w_in, tabs, aq_g, ak_g, bq_g, bkv_g, wuq, wuk, wuv):
    t = x.shape[0]
    tm = TOK_TILE
    row = lambda w: pl.BlockSpec((tm, w), lambda i: (i, 0))
    widths = (768, 256, 256, 1536, 1536, 768, 1024, 512, 512)
    return pl.pallas_call(
        _inproj_kernel,
        grid=(t // tm,),
        in_specs=[row(D_MODEL), _full(w_in.shape), row(LANES), row(LANES), row(LANES), row(LANES),
                  _full(aq_g.shape), _full(ak_g.shape), _full(bq_g.shape), _full(bkv_g.shape),
                  _full(wuq.shape), _full(wuk.shape), _full(wuv.shape)],
        out_specs=[row(w) for w in widths],
        out_shape=[jax.ShapeDtypeStruct((t, w), jnp.bfloat16) for w in widths],
        compiler_params=_cparams(("parallel",)),
        name="inproj",
    )(x, w_in, *tabs, aq_g, ak_g, bq_g, bkv_g, wuq, wuk, wuv)


def _flash_kernel(q_ref, k_ref, v_ref, o_ref, m_ref, l_ref, acc_ref, *, group, dq, scale):
    ki = pl.program_id(3)

    @pl.when(ki == 0)
    def _():
        m_ref[...] = jnp.full(m_ref.shape, -jnp.inf, jnp.float32)
        l_ref[...] = jnp.zeros(l_ref.shape, jnp.float32)
        acc_ref[...] = jnp.zeros(acc_ref.shape, jnp.float32)

    k = k_ref[...]
    v = v_ref[...]
    for g in range(group):
        q = q_ref[:, g * dq:(g + 1) * dq]
        s = lax.dot_general(q, k, _NT, preferred_element_type=jnp.float32) * scale
        m_prev = m_ref[g]
        m_new = jnp.maximum(m_prev, jnp.max(s, axis=-1, keepdims=True))
        alpha = jnp.exp(m_prev - m_new)
        p = jnp.exp(s - m_new)
        l_ref[g] = alpha * l_ref[g] + jnp.sum(p, axis=-1, keepdims=True)
        acc_ref[g] = alpha * acc_ref[g] + jnp.dot(p.astype(v.dtype), v, preferred_element_type=jnp.float32)
        m_ref[g] = m_new

    @pl.when(ki == pl.num_programs(3) - 1)
    def _():
        for g in range(group):
            o_ref[:, g * LANES:(g + 1) * LANES] = (acc_ref[g] / l_ref[g]).astype(o_ref.dtype)


def _flash(q, k, v, *, row0, nseq, seq, group, dq, scale, out_dtype):
    hkv = k.shape[1] // dq
    tq, tk = ATT_TQ, ATT_TK
    nq, nk = seq // tq, seq // tk
    q0, k0 = row0 // tq, row0 // tk
    qmap = lambda s, h, qi, ki: (q0 + s * nq + qi, h)
    kmap = lambda s, h, qi, ki: (k0 + s * nk + ki, h)
    omap = lambda s, h, qi, ki: (s * nq + qi, h)
    return pl.pallas_call(
        functools.partial(_flash_kernel, group=group, dq=dq, scale=scale),
        grid=(nseq, hkv, nq, nk),
        in_specs=[pl.BlockSpec((tq, group * dq), qmap), pl.BlockSpec((tk, dq), kmap),
                  pl.BlockSpec((tk, LANES), kmap)],
        out_specs=pl.BlockSpec((tq, group * LANES), omap),
        out_shape=jax.ShapeDtypeStruct((nseq * seq, hkv * group * LANES), out_dtype),
        scratch_shapes=[pltpu.VMEM((group, tq, 1), jnp.float32), pltpu.VMEM((group, tq, 1), jnp.float32),
                        pltpu.VMEM((group, tq, LANES), jnp.float32)],
        compiler_params=_cparams(("parallel", "parallel", "parallel", "arbitrary")),
        name="flash",
    )(q, k, v)


def _attend(q, k, v, seqs, **kw):
    outs = [_flash(q, k, v, row0=row0, nseq=nseq, seq=seq, **kw) for row0, nseq, seq in seqs]
    return jnp.concatenate(outs, axis=0)


def _outproj_kernel(oa_ref, ob_ref, oc_ref, x_ref, wo_ref, lamv_ref, subln_ref, g_ref, b_ref, y_ref,
                    *, lambda_init):
    bf = jnp.bfloat16
    lv = lamv_ref[...]
    lam = (jnp.exp(jnp.sum(lv[0:1] * lv[1:2], axis=-1, keepdims=True))
           - jnp.exp(jnp.sum(lv[2:3] * lv[3:4], axis=-1, keepdims=True)) + lambda_init)
    na, nb = A_HEADS * HEAD_DIM, B_HEADS * B_V
    mix = jnp.dot(oa_ref[...], wo_ref[0:na, :], preferred_element_type=jnp.float32)
    mix += jnp.dot(ob_ref[...], wo_ref[na:na + nb, :], preferred_element_type=jnp.float32)
    for h in range(C_HEADS):
        o1 = oc_ref[:, (2 * h) * LANES:(2 * h + 1) * LANES]
        o2 = oc_ref[:, (2 * h + 1) * LANES:(2 * h + 2) * LANES]
        c = (_rms(o1 - lam * o2, subln_ref[...]) * (1.0 - lambda_init)).astype(bf)
        lo = na + nb + h * C_V
        mix += jnp.dot(c, wo_ref[lo:lo + C_V, :], preferred_element_type=jnp.float32)
    y_ref[...] = _layer_norm(DN_ALPHA * x_ref[...] + mix, g_ref[...], b_ref[...])


def _outproj(oa, ob, oc, x, wo, lamv, subln, g, b, lambda_init):
    t = x.shape[0]
    tm = TOK_TILE
    row = lambda w: pl.BlockSpec((tm, w), lambda i: (i, 0))
    return pl.pallas_call(
        functools.partial(_outproj_kernel, lambda_init=lambda_init),
        grid=(t // tm,),
        in_specs=[row(oa.shape[1]), row(ob.shape[1]), row(oc.shape[1]), row(D_MODEL), _full(wo.shape),
                  _full(lamv.shape), _full(subln.shape), _full(g.shape), _full(b.shape)],
        out_specs=row(D_MODEL),
        out_shape=jax.ShapeDtypeStruct((t, D_MODEL), jnp.float32),
        compiler_params=_cparams(("parallel",)),
        name="outproj",
    )(oa, ob, oc, x, wo, lamv, subln, g, b)


def _topk_rows(s, payload, k):
    n = s.shape[0]
    iota = lax.broadcasted_iota(jnp.int32, s.shape, 0)
    vals, idxs, pays = [], [], []
    for _ in range(k):
        m = jnp.max(s, axis=0, keepdims=True)
        i = jnp.min(jnp.where(s == m, iota, n), axis=0, keepdims=True)
        hit = iota == i
        vals.append(m)
        idxs.append(i)
        if payload is not None:
            pays.append(jnp.sum(jnp.where(hit, payload, 0), axis=0, keepdims=True))
        s = jnp.where(hit, -jnp.inf, s)
    cat = lambda xs: jnp.concatenate(xs, axis=0)
    return cat(vals), cat(idxs), (cat(pays) if payload is not None else None)


def _route_kernel(x_ref, wq_ref, keys_ref, idx_ref, gate_ref):
    bf = jnp.bfloat16
    q = jnp.dot(x_ref[...].astype(bf), wq_ref[...], preferred_element_type=jnp.float32).astype(bf)
    half = PEER_QDIM // 2
    ids, gates = [], []
    for h in range(PEER_HEADS):
        tops = []
        for m in range(2):
            lo = (h * 2 + m) * half
            s = lax.dot_general(keys_ref[m], q[:, lo:lo + half], _NT, preferred_element_type=jnp.float32)
            v, i, _ = _topk_rows(s, None, PEER_TOPK)
            tops.append((v, i))
        (v1, i1), (v2, i2) = tops
        cand = jnp.concatenate([v1[a:a + 1] + v2 for a in range(PEER_TOPK)], axis=0)
        cand_id = jnp.concatenate([i1[a:a + 1] * PEER_NKEYS + i2 for a in range(PEER_TOPK)], axis=0)
        top_s, _, top_id = _topk_rows(cand, cand_id, PEER_TOPK)
        e = jnp.exp(top_s - top_s[0:1])
        gates.append(e / jnp.sum(e, axis=0, keepdims=True))
        ids.append(top_id)
    ids = jnp.concatenate(ids, axis=0)
    gates = jnp.concatenate(gates, axis=0)
    idx_ref[...] = ids.astype(jnp.float32).T.astype(jnp.int32)
    gate_ref[...] = gates.T


def _route(x, wq, keys):
    t = x.shape[0]
    tm = TOK_TILE
    row = lambda w: pl.BlockSpec((tm, w), lambda i: (i, 0))
    return pl.pallas_call(
        _route_kernel,
        grid=(t // tm,),
        in_specs=[row(D_MODEL), _full(wq.shape), _full(keys.shape)],
        out_specs=[row(PEER_SLOTS), row(PEER_SLOTS)],
        out_shape=[jax.ShapeDtypeStruct((t, PEER_SLOTS), jnp.int32),
                   jax.ShapeDtypeStruct((t, PEER_SLOTS), jnp.float32)],
        compiler_params=_cparams(("parallel",)),
        name="route",
    )(x, wq, keys)


def _peer_kernel(idx_ref, x_ref, gate_ref, g_ref, b_ref, tab_ref, y_ref, buf_ref, sem_ref):
    bf = jnp.bfloat16
    grp = PEER_GROUP
    rows = grp * PEER_SLOTS
    ngroups = x_ref.shape[0] // grp

    def row_copy(slot, tok, j, r):
        return pltpu.make_async_copy(tab_ref.at[pl.ds(idx_ref[tok, r], 1), :],
                                     buf_ref.at[slot, pl.ds(j * PEER_SLOTS + r, 1), :],
                                     sem_ref.at[slot])

    def start_group(gi, slot):
        def per_token(j, carry):
            for r in range(PEER_SLOTS):
                row_copy(slot, gi * grp + j, j, r).start()
            return carry
        lax.fori_loop(0, grp, per_token, 0)

    def wait_group(slot):
        pltpu.make_async_copy(buf_ref.at[slot], buf_ref.at[slot], sem_ref.at[slot]).wait()

    lane_blk = lax.broadcasted_iota(jnp.int32, (grp, rows), 1) // PEER_SLOTS
    own = lane_blk == lax.broadcasted_iota(jnp.int32, (grp, rows), 0)

    start_group(0, 0)

    def body(gi, carry):
        slot = gi % 2

        @pl.when(gi + 1 < ngroups)
        def _():
            start_group(gi + 1, 1 - slot)

        wait_group(slot)
        r0 = pl.multiple_of(gi * grp, grp)
        xg = x_ref[pl.ds(r0, grp), :].astype(bf)
        u = buf_ref[slot, :, 0:D_MODEL].astype(bf)
        act_all = lax.dot_general(xg, u, _NT, preferred_element_type=jnp.float32)
        act_all = jnp.where(own, act_all, 0.0)
        act = act_all[:, 0:PEER_SLOTS]
        for j in range(1, grp):
            act = act + act_all[:, j * PEER_SLOTS:(j + 1) * PEER_SLOTS]
        gelu = 0.5 * act * (1.0 + lax.erf(act * (2.0 ** -0.5)))
        w = gate_ref[pl.ds(r0, grp), :] * gelu
        w_all = jnp.where(own, jnp.concatenate([w] * grp, axis=1), 0.0).astype(bf)
        v = buf_ref[slot, :, D_MODEL:2 * D_MODEL].astype(bf)
        y_ref[pl.ds(r0, grp), :] = jnp.dot(w_all, v, preferred_element_type=jnp.float32)
        return carry

    lax.fori_loop(0, ngroups, body, 0)
    y_ref[...] = _layer_norm(DN_ALPHA * x_ref[...] + y_ref[...], g_ref[...], b_ref[...])


def _peer(idx, x, gate, g, b, table):
    t = x.shape[0]
    tm = PEER_TILE
    row = lambda w: pl.BlockSpec((tm, w), lambda i: (i, 0))
    return pl.pallas_call(
        _peer_kernel,
        grid=(t // tm,),
        in_specs=[pl.BlockSpec((tm, PEER_SLOTS), lambda i: (i, 0), memory_space=pltpu.SMEM),
                  row(D_MODEL), row(PEER_SLOTS), _full(g.shape), _full(b.shape),
                  pl.BlockSpec(memory_space=pl.ANY)],
        out_specs=row(D_MODEL),
        out_shape=jax.ShapeDtypeStruct((t, D_MODEL), jnp.float32),
        scratch_shapes=[pltpu.VMEM((2, PEER_GROUP * PEER_SLOTS, 2 * D_MODEL), jnp.float32),
                        pltpu.SemaphoreType.DMA((2,))],
        compiler_params=_cparams(("arbitrary",)),
        name="peer",
    )(idx, x, gate, g, b, table)


def _rope_tables(seq_lens):
    half = 32
    inv = jnp.power(ROPE_THETA, -jnp.arange(half, dtype=jnp.float32) / half)

    def table(p):
        ang = p.astype(jnp.float32)[:, None] * inv[None, :]
        c, s = jnp.cos(ang), jnp.sin(ang)
        return jnp.concatenate([c, c], axis=-1), jnp.concatenate([-s, s], axis=-1)

    parts = []
    for s in seq_lens:
        pos = jnp.arange(s, dtype=jnp.int32)
        (cr, sr), (cc, sc), (cp, sp) = table(pos // GRID_W), table(pos % GRID_W), table(pos)
        parts.append((jnp.concatenate([cr, cc], -1), jnp.concatenate([sr, sc], -1),
                      jnp.concatenate([cp, cp], -1), jnp.concatenate([sp, sp], -1)))
    return tuple(jnp.concatenate([p[i] for p in parts], axis=0) for i in range(4))


def _layer(x, seqs, tabs, lambda_init, w_in, a_qnorm_g, a_knorm_g, b_qnorm_g, b_kvnorm_g, b_w_uq, b_w_ukv,
           c_lambda_q1, c_lambda_k1, c_lambda_q2, c_lambda_k2, c_subln_g, w_o, ln1_g, ln1_b,
           peer_w_query, peer_sub_keys, peer_u, peer_v, ln2_g, ln2_b):
    bf = jnp.bfloat16
    row = lambda a: a.reshape(1, -1)
    w_in_p = jnp.concatenate([w_in[:, :_B_KR + B_ROPE], jnp.zeros((D_MODEL, LANES - B_ROPE), w_in.dtype),
                              w_in[:, _B_KR + B_ROPE:]], axis=1).astype(bf)
    wuq = b_w_uq.reshape(B_Q_RANK, B_HEADS, B_NOPE + B_ROPE)
    wuq = jnp.pad(wuq, ((0, 0), (0, 0), (0, 2 * LANES - B_NOPE - B_ROPE))).reshape(B_Q_RANK, -1).astype(bf)
    wukv = b_w_ukv.reshape(B_KV_RANK, B_HEADS, B_NOPE + B_V)
    wuk = wukv[:, :, :B_NOPE].reshape(B_KV_RANK, -1).astype(bf)
    wuv = wukv[:, :, B_NOPE:].reshape(B_KV_RANK, -1).astype(bf)
    lamv = jnp.stack([c_lambda_q1, c_lambda_k1, c_lambda_q2, c_lambda_k2], axis=0)
    table = jnp.concatenate([peer_u, peer_v], axis=1)

    qa, ka, va, qb, kb, vb, qc, kc, vc = _inproj(
        x, w_in_p, tabs, row(a_qnorm_g), row(a_knorm_g), row(b_qnorm_g), row(b_kvnorm_g), wuq, wuk, wuv)
    oa = _attend(qa, ka, va, seqs, group=A_HEADS // A_KV_HEADS, dq=HEAD_DIM,
                 scale=HEAD_DIM ** -0.5, out_dtype=bf)
    ob = _attend(qb, kb, vb, seqs, group=1, dq=2 * LANES,
                 scale=(B_NOPE + B_ROPE) ** -0.5, out_dtype=bf)
    oc = _attend(qc, kc, vc, seqs, group=2, dq=LANES, scale=C_QK ** -0.5, out_dtype=jnp.float32)
    x = _outproj(oa, ob, oc, x, w_o.astype(bf), lamv, row(c_subln_g), row(ln1_g), row(ln1_b), lambda_init)
    idx, gate = _route(x, peer_w_query.astype(bf), peer_sub_keys.astype(bf))
    return _peer(idx, x, gate, row(ln2_g), row(ln2_b), table)


def kernel(x_prompt, x_sample, w_in, a_qnorm_g, a_knorm_g, b_qnorm_g, b_kvnorm_g, b_w_uq, b_w_ukv,
           c_lambda_q1, c_lambda_k1, c_lambda_q2, c_lambda_k2, c_subln_g, w_o, ln1_g, ln1_b,
           peer_w_query, peer_sub_keys, peer_u, peer_v, ln2_g, ln2_b):
    nb_p, s_p, _ = x_prompt.shape
    nb_s, s_s, _ = x_sample.shape
    t_p = nb_p * s_p
    x = jnp.concatenate([x_prompt.reshape(t_p, D_MODEL), x_sample.reshape(nb_s * s_s, D_MODEL)], axis=0)
    seqs = ((0, nb_p, s_p), (t_p, nb_s, s_s))
    tabs = _rope_tables([s_p] * nb_p + [s_s] * nb_s)
    params = (w_in, a_qnorm_g, a_knorm_g, b_qnorm_g, b_kvnorm_g, b_w_uq, b_w_ukv, c_lambda_q1, c_lambda_k1,
              c_lambda_q2, c_lambda_k2, c_subln_g, w_o, ln1_g, ln1_b, peer_w_query, peer_sub_keys, peer_u,
              peer_v, ln2_g, ln2_b)
    for l in range(DEPTH):
        x = _layer(x, seqs, tabs, 0.8 - 0.6 * math.exp(-0.3 * l), *(p[l] for p in params))

    y_prompt = x[:t_p].reshape(nb_p, s_p, D_MODEL)
    y_sample = x[t_p:].reshape(nb_s, s_s, D_MODEL)
    return (y_prompt, y_sample)
```

```python
import functools
import math

import jax
import jax.numpy as jnp
from jax import lax
from jax.experimental import pallas as pl
from jax.experimental.pallas import tpu as pltpu

D_MODEL = 2048
DEPTH = 4
GRID_W = 64
ROPE_THETA = 10000.0
EPS = 1e-5
HEAD_DIM = 128
A_HEADS, A_KV_HEADS = 6, 2
B_HEADS, B_Q_RANK, B_KV_RANK, B_NOPE, B_ROPE, B_V = 6, 384, 256, 128, 64, 128
C_HEADS, C_QK = 4, 64
C_V = 2 * C_QK
PEER_HEADS, PEER_NKEYS, PEER_TOPK, PEER_QDIM = 8, 128, 16, 256
PEER_SLOTS = PEER_HEADS * PEER_TOPK
DN_ALPHA = (2 * DEPTH) ** 0.25

LANES = 128
SUBLANES = 8
VMEM_LIMIT = 56 * 1024 * 1024

TOK_TILE = 256
ATT_TQ = 512
ATT_TK = 1024
PEER_TILE = 128
PEER_GROUP = SUBLANES

_A_Q, _A_K, _A_V = 0, 768, 1024
_B_CQ, _B_CKV, _B_KR = 1280, 1664, 1920
_C_Q, _C_K, _C_V_OFF = 2048, 2560, 3072
IN_WIDTH_PAD = 3584

_NT = (((1,), (1,)), ((), ()))

_LOG2E = math.log2(math.e)
QSCALE_A = HEAD_DIM ** -0.5 * _LOG2E
QSCALE_B = (B_NOPE + B_ROPE) ** -0.5 * _LOG2E
QSCALE_C = C_QK ** -0.5 * _LOG2E


def _cparams(sem):
    return pltpu.CompilerParams(dimension_semantics=sem, vmem_limit_bytes=VMEM_LIMIT)


def _full(shape):
    return pl.BlockSpec(shape, lambda *_: (0,) * len(shape))


def _rms(x, g):
    return x * lax.rsqrt(jnp.mean(x * x, axis=-1, keepdims=True) + EPS) * g


def _layer_norm(y, g, b):
    mu = jnp.mean(y, axis=-1, keepdims=True)
    yc = y - mu
    var = jnp.mean(yc * yc, axis=-1, keepdims=True)
    return yc * lax.rsqrt(var + EPS) * g + b


def _rope64(x, cos, sin_signed):
    lane = lax.broadcasted_iota(jnp.int32, x.shape, 1)
    first = (lane % 64) < 32
    partner = jnp.where(first, pltpu.roll(x, 96, 1), pltpu.roll(x, 32, 1))
    return x * cos + partner * sin_signed


def _inproj_kernel(x_ref, w_ref, cosa_ref, sina_ref, cosp_ref, sinp_ref,
                   aq_g_ref, ak_g_ref, bq_g_ref, bkv_g_ref, wuq_ref, wuk_ref, wuv_ref,
                   qa_ref, ka_ref, va_ref, qb_ref, kb_ref, vb_ref, qc_ref, kc_ref, vc_ref):
    bf = jnp.bfloat16
    xb = x_ref[...].astype(bf)

    def proj(lo, hi):
        return jnp.dot(xb, w_ref[:, lo:hi], preferred_element_type=jnp.float32)

    cosa, sina = cosa_ref[...], sina_ref[...]
    cosp, sinp = cosp_ref[...], sinp_ref[...]

    hq = proj(_A_Q, _A_K)
    for h in range(A_HEADS):
        sl = slice(h * HEAD_DIM, (h + 1) * HEAD_DIM)
        qa_ref[:, sl] = (_rope64(_rms(hq[:, sl], aq_g_ref[...]), cosa, sina) * QSCALE_A).astype(bf)
    hk = proj(_A_K, _A_V)
    for h in range(A_KV_HEADS):
        sl = slice(h * HEAD_DIM, (h + 1) * HEAD_DIM)
        ka_ref[:, sl] = _rope64(_rms(hk[:, sl], ak_g_ref[...]), cosa, sina).astype(bf)
    va_ref[...] = proj(_A_V, _B_CQ).T.astype(bf)

    cq = _rms(proj(_B_CQ, _B_CKV), bq_g_ref[...]).astype(bf)
    qh = jnp.dot(cq, wuq_ref[...], preferred_element_type=jnp.float32) * QSCALE_B
    ckv = _rms(proj(_B_CKV, _B_KR), bkv_g_ref[...]).astype(bf)
    k_nope = jnp.dot(ckv, wuk_ref[...], preferred_element_type=jnp.float32)
    vb_ref[...] = jnp.dot(ckv, wuv_ref[...], preferred_element_type=jnp.float32).T.astype(bf)
    k_pe = _rope64(proj(_B_KR, _C_Q), cosp, sinp).astype(bf)
    for h in range(B_HEADS):
        lo = h * 2 * LANES
        qb_ref[:, lo:lo + LANES] = qh[:, lo:lo + LANES].astype(bf)
        qb_ref[:, lo + LANES:lo + 2 * LANES] = _rope64(qh[:, lo + LANES:lo + 2 * LANES], cosp, sinp).astype(bf)
        kb_ref[:, lo:lo + LANES] = k_nope[:, h * LANES:(h + 1) * LANES].astype(bf)
        kb_ref[:, lo + LANES:lo + 2 * LANES] = k_pe

    hq = proj(_C_Q, _C_K)
    lane = lax.broadcasted_iota(jnp.int32, (x_ref.shape[0], LANES), 1)
    for h in range(C_HEADS):
        sl = slice(h * LANES, (h + 1) * LANES)
        qr = _rope64(hq[:, sl], cosp, sinp) * QSCALE_C
        qc_ref[:, (2 * h) * LANES:(2 * h + 1) * LANES] = jnp.where(lane < C_QK, qr, 0.0).astype(bf)
        qc_ref[:, (2 * h + 1) * LANES:(2 * h + 2) * LANES] = jnp.where(lane < C_QK, 0.0, qr).astype(bf)
    hk = proj(_C_K, _C_V_OFF)
    for h in range(C_HEADS):
        sl = slice(h * LANES, (h + 1) * LANES)
        kc_ref[:, sl] = _rope64(hk[:, sl], cosp, sinp).astype(bf)
    vc_ref[...] = proj(_C_V_OFF, IN_WIDTH_PAD).T.astype(bf)


def _inproj(x, w_in, tabs, aq_g, ak_g, bq_g, bkv_g, wuq, wuk, wuv):
    t = x.shape[0]
    tm = TOK_TILE
    row = lambda w: pl.BlockSpec((tm, w), lambda i: (i, 0))
    col = lambda w: pl.BlockSpec((w, tm), lambda i: (0, i))
    widths = (768, 256, 256, 1536, 1536, 768, 1024, 512, 512)
    is_v = (False, False, True) * 3
    return pl.pallas_call(
        _inproj_kernel,
        grid=(t // tm,),
        in_specs=[row(D_MODEL), _full(w_in.shape), row(LANES), row(LANES), row(LANES), row(LANES),
                  _full(aq_g.shape), _full(ak_g.shape), _full(bq_g.shape), _full(bkv_g.shape),
                  _full(wuq.shape), _full(wuk.shape), _full(wuv.shape)],
        out_specs=[col(w) if tr else row(w) for w, tr in zip(widths, is_v)],
        out_shape=[jax.ShapeDtypeStruct((w, t) if tr else (t, w), jnp.bfloat16) for w, tr in zip(widths, is_v)],
        compiler_params=_cparams(("parallel",)),
        name="inproj",
    )(x, w_in, *tabs, aq_g, ak_g, bq_g, bkv_g, wuq, wuk, wuv)


def _flash_kernel(q_ref, k_ref, vt_ref, o_ref, m_ref, l_ref, acc_ref, *, group, dq):
    ki = pl.program_id(3)

    @pl.when(ki == 0)
    def _():
        m_ref[...] = jnp.full(m_ref.shape, -jnp.inf, jnp.float32)
        l_ref[...] = jnp.zeros(l_ref.shape, jnp.float32)
        acc_ref[...] = jnp.zeros(acc_ref.shape, jnp.float32)

    k = k_ref[...]
    vt = vt_ref[...]
    for g in range(group):
        q = q_ref[:, g * dq:(g + 1) * dq]
        st = lax.dot_general(k, q, _NT, preferred_element_type=jnp.float32)
        m_prev = m_ref[g]
        m_new = jnp.maximum(m_prev, jnp.max(st, axis=0, keepdims=True))
        alpha = jnp.exp2(m_prev - m_new)
        p = jnp.exp2(st - m_new)
        l_ref[g] = alpha * l_ref[g] + jnp.sum(p, axis=0, keepdims=True)
        acc_ref[g] = alpha * acc_ref[g] + jnp.dot(vt, p.astype(vt.dtype), preferred_element_type=jnp.float32)
        m_ref[g] = m_new

    @pl.when(ki == pl.num_programs(3) - 1)
    def _():
        for g in range(group):
            o_ref[:, g * LANES:(g + 1) * LANES] = (acc_ref[g] / l_ref[g]).T.astype(o_ref.dtype)


def _flash(q, k, vt, *, row0, nseq, seq, group, dq, out_dtype):
    hkv = k.shape[1] // dq
    tq, tk = ATT_TQ, ATT_TK
    nq, nk = seq // tq, seq // tk
    q0, k0 = row0 // tq, row0 // tk
    qmap = lambda s, h, qi, ki: (q0 + s * nq + qi, h)
    kmap = lambda s, h, qi, ki: (k0 + s * nk + ki, h)
    vmap = lambda s, h, qi, ki: (h, k0 + s * nk + ki)
    omap = lambda s, h, qi, ki: (s * nq + qi, h)
    return pl.pallas_call(
        functools.partial(_flash_kernel, group=group, dq=dq),
        grid=(nseq, hkv, nq, nk),
        in_specs=[pl.BlockSpec((tq, group * dq), qmap), pl.BlockSpec((tk, dq), kmap),
                  pl.BlockSpec((LANES, tk), vmap)],
        out_specs=pl.BlockSpec((tq, group * LANES), omap),
        out_shape=jax.ShapeDtypeStruct((nseq * seq, hkv * group * LANES), out_dtype),
        scratch_shapes=[pltpu.VMEM((group, 1, tq), jnp.float32), pltpu.VMEM((group, 1, tq), jnp.float32),
                        pltpu.VMEM((group, LANES, tq), jnp.float32)],
        compiler_params=_cparams(("parallel", "parallel", "parallel", "arbitrary")),
        name="flash",
    )(q, k, vt)


def _attend(q, k, v, seqs, **kw):
    outs = [_flash(q, k, v, row0=row0, nseq=nseq, seq=seq, **kw) for row0, nseq, seq in seqs]
    return jnp.concatenate(outs, axis=0)


def _outproj_kernel(oa_ref, ob_ref, oc_ref, x_ref, wo_ref, lamv_ref, subln_ref, g_ref, b_ref, y_ref,
                    *, lambda_init):
    bf = jnp.bfloat16
    lv = lamv_ref[...]
    lam = (jnp.exp(jnp.sum(lv[0:1] * lv[1:2], axis=-1, keepdims=True))
           - jnp.exp(jnp.sum(lv[2:3] * lv[3:4], axis=-1, keepdims=True)) + lambda_init)
    na, nb = A_HEADS * HEAD_DIM, B_HEADS * B_V
    mix = jnp.dot(oa_ref[...], wo_ref[0:na, :], preferred_element_type=jnp.float32)
    mix += jnp.dot(ob_ref[...], wo_ref[na:na + nb, :], preferred_element_type=jnp.float32)
    for h in range(C_HEADS):
        o1 = oc_ref[:, (2 * h) * LANES:(2 * h + 1) * LANES]
        o2 = oc_ref[:, (2 * h + 1) * LANES:(2 * h + 2) * LANES]
        c = (_rms(o1 - lam * o2, subln_ref[...]) * (1.0 - lambda_init)).astype(bf)
        lo = na + nb + h * C_V
        mix += jnp.dot(c, wo_ref[lo:lo + C_V, :], preferred_element_type=jnp.float32)
    y_ref[...] = _layer_norm(DN_ALPHA * x_ref[...] + mix, g_ref[...], b_ref[...])


def _outproj(oa, ob, oc, x, wo, lamv, subln, g, b, lambda_init):
    t = x.shape[0]
    tm = TOK_TILE
    row = lambda w: pl.BlockSpec((tm, w), lambda i: (i, 0))
    return pl.pallas_call(
        functools.partial(_outproj_kernel, lambda_init=lambda_init),
        grid=(t // tm,),
        in_specs=[row(oa.shape[1]), row(ob.shape[1]), row(oc.shape[1]), row(D_MODEL), _full(wo.shape),
                  _full(lamv.shape), _full(subln.shape), _full(g.shape), _full(b.shape)],
        out_specs=row(D_MODEL),
        out_shape=jax.ShapeDtypeStruct((t, D_MODEL), jnp.float32),
        compiler_params=_cparams(("parallel",)),
        name="outproj",
    )(oa, ob, oc, x, wo, lamv, subln, g, b)


def _topk_rows(s, payload, k):
    n = s.shape[0]
    iota = lax.broadcasted_iota(jnp.int32, s.shape, 0)
    vals, idxs, pays = [], [], []
    for _ in range(k):
        m = jnp.max(s, axis=0, keepdims=True)
        i = jnp.min(jnp.where(s == m, iota, n), axis=0, keepdims=True)
        hit = iota == i
        vals.append(m)
        idxs.append(i)
        if payload is not None:
            pays.append(jnp.sum(jnp.where(hit, payload, 0), axis=0, keepdims=True))
        s = jnp.where(hit, -jnp.inf, s)
    cat = lambda xs: jnp.concatenate(xs, axis=0)
    return cat(vals), cat(idxs), (cat(pays) if payload is not None else None)


def _route_kernel(x_ref, wq_ref, keys_ref, idx_ref, gate_ref):
    bf = jnp.bfloat16
    q = jnp.dot(x_ref[...].astype(bf), wq_ref[...], preferred_element_type=jnp.float32).astype(bf)
    half = PEER_QDIM // 2
    ids, gates = [], []
    for h in range(PEER_HEADS):
        tops = []
        for m in range(2):
            lo = (h * 2 + m) * half
            s = lax.dot_general(keys_ref[m], q[:, lo:lo + half], _NT, preferred_element_type=jnp.float32)
            v, i, _ = _topk_rows(s, None, PEER_TOPK)
            tops.append((v, i))
        (v1, i1), (v2, i2) = tops
        cand = jnp.concatenate([v1[a:a + 1] + v2 for a in range(PEER_TOPK)], axis=0)
        cand_id = jnp.concatenate([i1[a:a + 1] * PEER_NKEYS + i2 for a in range(PEER_TOPK)], axis=0)
        top_s, _, top_id = _topk_rows(cand, cand_id, PEER_TOPK)
        e = jnp.exp(top_s - top_s[0:1])
        gates.append(e / jnp.sum(e, axis=0, keepdims=True))
        ids.append(top_id)
    ids = jnp.concatenate(ids, axis=0)
    gates = jnp.concatenate(gates, axis=0)
    idx_ref[...] = ids.astype(jnp.float32).T.astype(jnp.int32)
    gate_ref[...] = gates.T


def _route(x, wq, keys):
    t = x.shape[0]
    tm = TOK_TILE
    row = lambda w: pl.BlockSpec((tm, w), lambda i: (i, 0))
    return pl.pallas_call(
        _route_kernel,
        grid=(t // tm,),
        in_specs=[row(D_MODEL), _full(wq.shape), _full(keys.shape)],
        out_specs=[row(PEER_SLOTS), row(PEER_SLOTS)],
        out_shape=[jax.ShapeDtypeStruct((t, PEER_SLOTS), jnp.int32),
                   jax.ShapeDtypeStruct((t, PEER_SLOTS), jnp.float32)],
        compiler_params=_cparams(("parallel",)),
        name="route",
    )(x, wq, keys)


def _peer_kernel(idx_ref, x_ref, gate_ref, g_ref, b_ref, tab_ref, y_ref, buf_ref, sem_ref):
    bf = jnp.bfloat16
    grp = PEER_GROUP
    rows = grp * PEER_SLOTS
    ngroups = x_ref.shape[0] // grp

    def row_copy(slot, tok, j, r):
        return pltpu.make_async_copy(tab_ref.at[pl.ds(idx_ref[tok, r], 1), :],
                                     buf_ref.at[slot, pl.ds(j * PEER_SLOTS + r, 1), :],
                                     sem_ref.at[slot])

    def start_group(gi, slot):
        def per_token(j, carry):
            for r in range(PEER_SLOTS):
                row_copy(slot, gi * grp + j, j, r).start()
            return carry
        lax.fori_loop(0, grp, per_token, 0)

    def wait_group(slot):
        pltpu.make_async_copy(buf_ref.at[slot], buf_ref.at[slot], sem_ref.at[slot]).wait()

    lane_blk = lax.broadcasted_iota(jnp.int32, (grp, rows), 1) // PEER_SLOTS
    own = lane_blk == lax.broadcasted_iota(jnp.int32, (grp, rows), 0)

    start_group(0, 0)

    def body(gi, carry):
        slot = gi % 2

        @pl.when(gi + 1 < ngroups)
        def _():
            start_group(gi + 1, 1 - slot)

        wait_group(slot)
        r0 = pl.multiple_of(gi * grp, grp)
        xg = x_ref[pl.ds(r0, grp), :].astype(bf)
        u = buf_ref[slot, :, 0:D_MODEL].astype(bf)
        act_all = lax.dot_general(xg, u, _NT, preferred_element_type=jnp.float32)
        act_all = jnp.where(own, act_all, 0.0)
        act = act_all[:, 0:PEER_SLOTS]
        for j in range(1, grp):
            act = act + act_all[:, j * PEER_SLOTS:(j + 1) * PEER_SLOTS]
        gelu = 0.5 * act * (1.0 + lax.erf(act * (2.0 ** -0.5)))
        w = gate_ref[pl.ds(r0, grp), :] * gelu
        w_all = jnp.where(own, jnp.concatenate([w] * grp, axis=1), 0.0).astype(bf)
        v = buf_ref[slot, :, D_MODEL:2 * D_MODEL].astype(bf)
        y_ref[pl.ds(r0, grp), :] = jnp.dot(w_all, v, preferred_element_type=jnp.float32)
        return carry

    lax.fori_loop(0, ngroups, body, 0)
    y_ref[...] = _layer_norm(DN_ALPHA * x_ref[...] + y_ref[...], g_ref[...], b_ref[...])


def _peer(idx, x, gate, g, b, table):
    t = x.shape[0]
    tm = PEER_TILE
    row = lambda w: pl.BlockSpec((tm, w), lambda i: (i, 0))
    return pl.pallas_call(
        _peer_kernel,
        grid=(t // tm,),
        in_specs=[pl.BlockSpec((tm, PEER_SLOTS), lambda i: (i, 0), memory_space=pltpu.SMEM),
                  row(D_MODEL), row(PEER_SLOTS), _full(g.shape), _full(b.shape),
                  pl.BlockSpec(memory_space=pl.ANY)],
        out_specs=row(D_MODEL),
        out_shape=jax.ShapeDtypeStruct((t, D_MODEL), jnp.float32),
        scratch_shapes=[pltpu.VMEM((2, PEER_GROUP * PEER_SLOTS, 2 * D_MODEL), jnp.float32),
                        pltpu.SemaphoreType.DMA((2,))],
        compiler_params=_cparams(("arbitrary",)),
        name="peer",
    )(idx, x, gate, g, b, table)


def _rope_tables(seq_lens):
    half = 32
    inv = jnp.power(ROPE_THETA, -jnp.arange(half, dtype=jnp.float32) / half)

    def table(p):
        ang = p.astype(jnp.float32)[:, None] * inv[None, :]
        c, s = jnp.cos(ang), jnp.sin(ang)
        return jnp.concatenate([c, c], axis=-1), jnp.concatenate([-s, s], axis=-1)

    parts = []
    for s in seq_lens:
        pos = jnp.arange(s, dtype=jnp.int32)
        (cr, sr), (cc, sc), (cp, sp) = table(pos // GRID_W), table(pos % GRID_W), table(pos)
        parts.append((jnp.concatenate([cr, cc], -1), jnp.concatenate([sr, sc], -1),
                      jnp.concatenate([cp, cp], -1), jnp.concatenate([sp, sp], -1)))
    return tuple(jnp.concatenate([p[i] for p in parts], axis=0) for i in range(4))


def _layer(x, seqs, tabs, lambda_init, w_in, a_qnorm_g, a_knorm_g, b_qnorm_g, b_kvnorm_g, b_w_uq, b_w_ukv,
           c_lambda_q1, c_lambda_k1, c_lambda_q2, c_lambda_k2, c_subln_g, w_o, ln1_g, ln1_b,
           peer_w_query, peer_sub_keys, peer_u, peer_v, ln2_g, ln2_b):
    bf = jnp.bfloat16
    row = lambda a: a.reshape(1, -1)
    w_in_p = jnp.concatenate([w_in[:, :_B_KR + B_ROPE], jnp.zeros((D_MODEL, LANES - B_ROPE), w_in.dtype),
                              w_in[:, _B_KR + B_ROPE:]], axis=1).astype(bf)
    wuq = b_w_uq.reshape(B_Q_RANK, B_HEADS, B_NOPE + B_ROPE)
    wuq = jnp.pad(wuq, ((0, 0), (0, 0), (0, 2 * LANES - B_NOPE - B_ROPE))).reshape(B_Q_RANK, -1).astype(bf)
    wukv = b_w_ukv.reshape(B_KV_RANK, B_HEADS, B_NOPE + B_V)
    wuk = wukv[:, :, :B_NOPE].reshape(B_KV_RANK, -1).astype(bf)
    wuv = wukv[:, :, B_NOPE:].reshape(B_KV_RANK, -1).astype(bf)
    lamv = jnp.stack([c_lambda_q1, c_lambda_k1, c_lambda_q2, c_lambda_k2], axis=0)
    table = jnp.concatenate([peer_u, peer_v], axis=1)

    qa, ka, va, qb, kb, vb, qc, kc, vc = _inproj(
        x, w_in_p, tabs, row(a_qnorm_g), row(a_knorm_g), row(b_qnorm_g), row(b_kvnorm_g), wuq, wuk, wuv)
    oa = _attend(qa, ka, va, seqs, group=A_HEADS // A_KV_HEADS, dq=HEAD_DIM, out_dtype=bf)
    ob = _attend(qb, kb, vb, seqs, group=1, dq=2 * LANES, out_dtype=bf)
    oc = _attend(qc, kc, vc, seqs, group=2, dq=LANES, out_dtype=jnp.float32)
    x = _outproj(oa, ob, oc, x, w_o.astype(bf), lamv, row(c_subln_g), row(ln1_g), row(ln1_b), lambda_init)
    idx, gate = _route(x, peer_w_query.astype(bf), peer_sub_keys.astype(bf))
    return _peer(idx, x, gate, row(ln2_g), row(ln2_b), table)


def kernel(x_prompt, x_sample, w_in, a_qnorm_g, a_knorm_g, b_qnorm_g, b_kvnorm_g, b_w_uq, b_w_ukv,
           c_lambda_q1, c_lambda_k1, c_lambda_q2, c_lambda_k2, c_subln_g, w_o, ln1_g, ln1_b,
           peer_w_query, peer_sub_keys, peer_u, peer_v, ln2_g, ln2_b):
    nb_p, s_p, _ = x_prompt.shape
    nb_s, s_s, _ = x_sample.shape
    t_p = nb_p * s_p
    x = jnp.concatenate([x_prompt.reshape(t_p, D_MODEL), x_sample.reshape(nb_s * s_s, D_MODEL)], axis=0)
    seqs = ((0, nb_p, s_p), (t_p, nb_s, s_s))
    tabs = _rope_tables([s_p] * nb_p + [s_s] * nb_s)
    params = (w_in, a_qnorm_g, a_knorm_g, b_qnorm_g, b_kvnorm_g, b_w_uq, b_w_ukv, c_lambda_q1, c_lambda_k1,
              c_lambda_q2, c_lambda_k2, c_subln_g, w_o, ln1_g, ln1_b, peer_w_query, peer_sub_keys, peer_u,
              peer_v, ln2_g, ln2_b)
    for l in range(DEPTH):
        x = _layer(x, seqs, tabs, 0.8 - 0.6 * math.exp(-0.3 * l), *(p[l] for p in params))

    y_prompt = x[:t_p].reshape(nb_p, s_p, D_MODEL)
    y_sample = x[t_p:].reshape(nb_s, s_s, D_MODEL)
    return (y_prompt, y_sample)
```

```python
import functools
import math

import jax
import jax.numpy as jnp
from jax import lax
from jax.experimental import pallas as pl
from jax.experimental.pallas import tpu as pltpu

D_MODEL = 2048
DEPTH = 4
GRID_W = 64
ROPE_THETA = 10000.0
EPS = 1e-5
HEAD_DIM = 128
A_HEADS, A_KV_HEADS = 6, 2
B_HEADS, B_Q_RANK, B_KV_RANK, B_NOPE, B_ROPE, B_V = 6, 384, 256, 128, 64, 128
C_HEADS, C_QK = 4, 64
C_V = 2 * C_QK
PEER_HEADS, PEER_NKEYS, PEER_TOPK, PEER_QDIM = 8, 128, 16, 256
PEER_SLOTS = PEER_HEADS * PEER_TOPK
DN_ALPHA = (2 * DEPTH) ** 0.25

LANES = 128
SUBLANES = 8
VMEM_LIMIT = 56 * 1024 * 1024

TOK_TILE = 256
ATT_TQ = 512
ATT_TK = 1024
PEER_TILE = 256
PEER_GROUP = SUBLANES

_A_Q, _A_K, _A_V = 0, 768, 1024
_B_CQ, _B_CKV, _B_KR = 1280, 1664, 1920
_C_Q, _C_K, _C_V_OFF = 2048, 2560, 3072
IN_WIDTH_PAD = 3584

_NT = (((1,), (1,)), ((), ()))

_LOG2E = math.log2(math.e)
QSCALE_A = HEAD_DIM ** -0.5 * _LOG2E
QSCALE_B = (B_NOPE + B_ROPE) ** -0.5 * _LOG2E
QSCALE_C = C_QK ** -0.5 * _LOG2E


def _cparams(sem):
    return pltpu.CompilerParams(dimension_semantics=sem, vmem_limit_bytes=VMEM_LIMIT)


def _full(shape):
    return pl.BlockSpec(shape, lambda *_: (0,) * len(shape))


def _rms(x, g):
    return x * lax.rsqrt(jnp.mean(x * x, axis=-1, keepdims=True) + EPS) * g


def _layer_norm(y, g, b):
    mu = jnp.mean(y, axis=-1, keepdims=True)
    yc = y - mu
    var = jnp.mean(yc * yc, axis=-1, keepdims=True)
    return yc * lax.rsqrt(var + EPS) * g + b


def _rope64(x, cos, sin_signed):
    lane = lax.broadcasted_iota(jnp.int32, x.shape, 1)
    first = (lane % 64) < 32
    partner = jnp.where(first, pltpu.roll(x, 96, 1), pltpu.roll(x, 32, 1))
    return x * cos + partner * sin_signed


def _inproj_kernel(x_ref, w_ref, cosa_ref, sina_ref, cosp_ref, sinp_ref,
                   aq_g_ref, ak_g_ref, bq_g_ref, bkv_g_ref, wuq_ref, wuk_ref, wuv_ref,
                   qa_ref, ka_ref, va_ref, qb_ref, kb_ref, vb_ref, qc_ref, kc_ref, vc_ref):
    bf = jnp.bfloat16
    xb = x_ref[...].astype(bf)

    def proj(lo, hi):
        return jnp.dot(xb, w_ref[:, lo:hi], preferred_element_type=jnp.float32)

    cosa, sina = cosa_ref[...], sina_ref[...]
    cosp, sinp = cosp_ref[...], sinp_ref[...]

    hq = proj(_A_Q, _A_K)
    for h in range(A_HEADS):
        sl = slice(h * HEAD_DIM, (h + 1) * HEAD_DIM)
        qa_ref[:, sl] = (_rope64(_rms(hq[:, sl], aq_g_ref[...]), cosa, sina) * QSCALE_A).astype(bf)
    hk = proj(_A_K, _A_V)
    for h in range(A_KV_HEADS):
        sl = slice(h * HEAD_DIM, (h + 1) * HEAD_DIM)
        ka_ref[:, sl] = _rope64(_rms(hk[:, sl], ak_g_ref[...]), cosa, sina).astype(bf)
    va_ref[...] = proj(_A_V, _B_CQ).T.astype(bf)

    cq = _rms(proj(_B_CQ, _B_CKV), bq_g_ref[...]).astype(bf)
    qh = jnp.dot(cq, wuq_ref[...], preferred_element_type=jnp.float32) * QSCALE_B
    ckv = _rms(proj(_B_CKV, _B_KR), bkv_g_ref[...]).astype(bf)
    k_nope = jnp.dot(ckv, wuk_ref[...], preferred_element_type=jnp.float32)
    vb_ref[...] = jnp.dot(ckv, wuv_ref[...], preferred_element_type=jnp.float32).T.astype(bf)
    k_pe = _rope64(proj(_B_KR, _C_Q), cosp, sinp).astype(bf)
    for h in range(B_HEADS):
        lo = h * 2 * LANES
        qb_ref[:, lo:lo + LANES] = qh[:, lo:lo + LANES].astype(bf)
        qb_ref[:, lo + LANES:lo + 2 * LANES] = _rope64(qh[:, lo + LANES:lo + 2 * LANES], cosp, sinp).astype(bf)
        kb_ref[:, lo:lo + LANES] = k_nope[:, h * LANES:(h + 1) * LANES].astype(bf)
        kb_ref[:, lo + LANES:lo + 2 * LANES] = k_pe

    hq = proj(_C_Q, _C_K)
    lane = lax.broadcasted_iota(jnp.int32, (x_ref.shape[0], LANES), 1)
    for h in range(C_HEADS):
        sl = slice(h * LANES, (h + 1) * LANES)
        qr = _rope64(hq[:, sl], cosp, sinp) * QSCALE_C
        qc_ref[:, (2 * h) * LANES:(2 * h + 1) * LANES] = jnp.where(lane < C_QK, qr, 0.0).astype(bf)
        qc_ref[:, (2 * h + 1) * LANES:(2 * h + 2) * LANES] = jnp.where(lane < C_QK, 0.0, qr).astype(bf)
    hk = proj(_C_K, _C_V_OFF)
    for h in range(C_HEADS):
        sl = slice(h * LANES, (h + 1) * LANES)
        kc_ref[:, sl] = _rope64(hk[:, sl], cosp, sinp).astype(bf)
    vc_ref[...] = proj(_C_V_OFF, IN_WIDTH_PAD).T.astype(bf)


def _inproj(x, w_in, tabs, aq_g, ak_g, bq_g, bkv_g, wuq, wuk, wuv):
    t = x.shape[0]
    tm = TOK_TILE
    row = lambda w: pl.BlockSpec((tm, w), lambda i: (i, 0))
    col = lambda w: pl.BlockSpec((w, tm), lambda i: (0, i))
    widths = (768, 256, 256, 1536, 1536, 768, 1024, 512, 512)
    is_v = (False, False, True) * 3
    return pl.pallas_call(
        _inproj_kernel,
        grid=(t // tm,),
        in_specs=[row(D_MODEL), _full(w_in.shape), row(LANES), row(LANES), row(LANES), row(LANES),
                  _full(aq_g.shape), _full(ak_g.shape), _full(bq_g.shape), _full(bkv_g.shape),
                  _full(wuq.shape), _full(wuk.shape), _full(wuv.shape)],
        out_specs=[col(w) if tr else row(w) for w, tr in zip(widths, is_v)],
        out_shape=[jax.ShapeDtypeStruct((w, t) if tr else (t, w), jnp.bfloat16) for w, tr in zip(widths, is_v)],
        compiler_params=_cparams(("parallel",)),
        name="inproj",
    )(x, w_in, *tabs, aq_g, ak_g, bq_g, bkv_g, wuq, wuk, wuv)


def _flash_kernel(q_ref, k_ref, vt_ref, o_ref, m_ref, l_ref, acc_ref, *, group, dq):
    ki = pl.program_id(3)

    @pl.when(ki == 0)
    def _():
        m_ref[...] = jnp.full(m_ref.shape, -jnp.inf, jnp.float32)
        l_ref[...] = jnp.zeros(l_ref.shape, jnp.float32)
        acc_ref[...] = jnp.zeros(acc_ref.shape, jnp.float32)

    k = k_ref[...]
    vt = vt_ref[...]
    for g in range(group):
        q = q_ref[:, g * dq:(g + 1) * dq]
        st = lax.dot_general(k, q, _NT, preferred_element_type=jnp.float32)
        m_prev = m_ref[g]
        m_new = jnp.maximum(m_prev, jnp.max(st, axis=0, keepdims=True))
        alpha = jnp.exp2(m_prev - m_new)
        p = jnp.exp2(st - m_new)
        l_ref[g] = alpha * l_ref[g] + jnp.sum(p, axis=0, keepdims=True)
        acc_ref[g] = alpha * acc_ref[g] + jnp.dot(vt, p.astype(vt.dtype), preferred_element_type=jnp.float32)
        m_ref[g] = m_new

    @pl.when(ki == pl.num_programs(3) - 1)
    def _():
        for g in range(group):
            o_ref[:, g * LANES:(g + 1) * LANES] = (acc_ref[g] / l_ref[g]).T.astype(o_ref.dtype)


def _flash(q, k, vt, *, row0, nseq, seq, group, dq, out_dtype):
    hkv = k.shape[1] // dq
    tq, tk = ATT_TQ, ATT_TK
    nq, nk = seq // tq, seq // tk
    q0, k0 = row0 // tq, row0 // tk
    qmap = lambda s, h, qi, ki: (q0 + s * nq + qi, h)
    kmap = lambda s, h, qi, ki: (k0 + s * nk + ki, h)
    vmap = lambda s, h, qi, ki: (h, k0 + s * nk + ki)
    omap = lambda s, h, qi, ki: (s * nq + qi, h)
    return pl.pallas_call(
        functools.partial(_flash_kernel, group=group, dq=dq),
        grid=(nseq, hkv, nq, nk),
        in_specs=[pl.BlockSpec((tq, group * dq), qmap), pl.BlockSpec((tk, dq), kmap),
                  pl.BlockSpec((LANES, tk), vmap)],
        out_specs=pl.BlockSpec((tq, group * LANES), omap),
        out_shape=jax.ShapeDtypeStruct((nseq * seq, hkv * group * LANES), out_dtype),
        scratch_shapes=[pltpu.VMEM((group, 1, tq), jnp.float32), pltpu.VMEM((group, 1, tq), jnp.float32),
                        pltpu.VMEM((group, LANES, tq), jnp.float32)],
        compiler_params=_cparams(("parallel", "parallel", "parallel", "arbitrary")),
        name="flash",
    )(q, k, vt)


def _attend(q, k, v, seqs, **kw):
    outs = [_flash(q, k, v, row0=row0, nseq=nseq, seq=seq, **kw) for row0, nseq, seq in seqs]
    return jnp.concatenate(outs, axis=0)


def _outproj_kernel(oa_ref, ob_ref, oc_ref, x_ref, wo_ref, lamv_ref, subln_ref, g_ref, b_ref, y_ref,
                    *, lambda_init):
    bf = jnp.bfloat16
    lv = lamv_ref[...]
    lam = (jnp.exp(jnp.sum(lv[0:1] * lv[1:2], axis=-1, keepdims=True))
           - jnp.exp(jnp.sum(lv[2:3] * lv[3:4], axis=-1, keepdims=True)) + lambda_init)
    na, nb = A_HEADS * HEAD_DIM, B_HEADS * B_V
    mix = jnp.dot(oa_ref[...], wo_ref[0:na, :], preferred_element_type=jnp.float32)
    mix += jnp.dot(ob_ref[...], wo_ref[na:na + nb, :], preferred_element_type=jnp.float32)
    for h in range(C_HEADS):
        o1 = oc_ref[:, (2 * h) * LANES:(2 * h + 1) * LANES]
        o2 = oc_ref[:, (2 * h + 1) * LANES:(2 * h + 2) * LANES]
        c = (_rms(o1 - lam * o2, subln_ref[...]) * (1.0 - lambda_init)).astype(bf)
        lo = na + nb + h * C_V
        mix += jnp.dot(c, wo_ref[lo:lo + C_V, :], preferred_element_type=jnp.float32)
    y_ref[...] = _layer_norm(DN_ALPHA * x_ref[...] + mix, g_ref[...], b_ref[...])


def _outproj(oa, ob, oc, x, wo, lamv, subln, g, b, lambda_init):
    t = x.shape[0]
    tm = TOK_TILE
    row = lambda w: pl.BlockSpec((tm, w), lambda i: (i, 0))
    return pl.pallas_call(
        functools.partial(_outproj_kernel, lambda_init=lambda_init),
        grid=(t // tm,),
        in_specs=[row(oa.shape[1]), row(ob.shape[1]), row(oc.shape[1]), row(D_MODEL), _full(wo.shape),
                  _full(lamv.shape), _full(subln.shape), _full(g.shape), _full(b.shape)],
        out_specs=row(D_MODEL),
        out_shape=jax.ShapeDtypeStruct((t, D_MODEL), jnp.float32),
        compiler_params=_cparams(("parallel",)),
        name="outproj",
    )(oa, ob, oc, x, wo, lamv, subln, g, b)


def _topk_rows(s, payload, k):
    n = s.shape[0]
    iota = lax.broadcasted_iota(jnp.int32, s.shape, 0)
    vals, idxs, pays = [], [], []
    for _ in range(k):
        m = jnp.max(s, axis=0, keepdims=True)
        i = jnp.min(jnp.where(s == m, iota, n), axis=0, keepdims=True)
        hit = iota == i
        vals.append(m)
        idxs.append(i)
        if payload is not None:
            pays.append(jnp.sum(jnp.where(hit, payload, 0), axis=0, keepdims=True))
        s = jnp.where(hit, -jnp.inf, s)
    cat = lambda xs: jnp.concatenate(xs, axis=0)
    return cat(vals), cat(idxs), (cat(pays) if payload is not None else None)


def _route_kernel(x_ref, wq_ref, keys_ref, idx_ref, gate_ref):
    bf = jnp.bfloat16
    q = jnp.dot(x_ref[...].astype(bf), wq_ref[...], preferred_element_type=jnp.float32).astype(bf)
    half = PEER_QDIM // 2
    ids, gates = [], []
    for h in range(PEER_HEADS):
        tops = []
        for m in range(2):
            lo = (h * 2 + m) * half
            s = lax.dot_general(keys_ref[m], q[:, lo:lo + half], _NT, preferred_element_type=jnp.float32)
            v, i, _ = _topk_rows(s, None, PEER_TOPK)
            tops.append((v, i))
        (v1, i1), (v2, i2) = tops
        cand = jnp.concatenate([v1[a:a + 1] + v2 for a in range(PEER_TOPK)], axis=0)
        cand_id = jnp.concatenate([i1[a:a + 1] * PEER_NKEYS + i2 for a in range(PEER_TOPK)], axis=0)
        top_s, _, top_id = _topk_rows(cand, cand_id, PEER_TOPK)
        e = jnp.exp(top_s - top_s[0:1])
        gates.append(e / jnp.sum(e, axis=0, keepdims=True))
        ids.append(top_id)
    ids = jnp.concatenate(ids, axis=0)
    gates = jnp.concatenate(gates, axis=0)
    idx_ref[...] = ids.astype(jnp.float32).T.astype(jnp.int32)
    gate_ref[...] = gates.T


def _route(x, wq, keys):
    t = x.shape[0]
    tm = TOK_TILE
    row = lambda w: pl.BlockSpec((tm, w), lambda i: (i, 0))
    return pl.pallas_call(
        _route_kernel,
        grid=(t // tm,),
        in_specs=[row(D_MODEL), _full(wq.shape), _full(keys.shape)],
        out_specs=[row(PEER_SLOTS), row(PEER_SLOTS)],
        out_shape=[jax.ShapeDtypeStruct((t, PEER_SLOTS), jnp.int32),
                   jax.ShapeDtypeStruct((t, PEER_SLOTS), jnp.float32)],
        compiler_params=_cparams(("parallel",)),
        name="route",
    )(x, wq, keys)


def _peer_kernel(idx_ref, x_ref, gate_ref, g_ref, b_ref, tab_ref, y_ref, buf0_ref, buf1_ref, sem_ref):
    bf = jnp.bfloat16
    grp = PEER_GROUP
    rows = grp * PEER_SLOTS
    ngroups = x_ref.shape[0] // grp
    bufs = (buf0_ref, buf1_ref)

    def start_group(gi, slot):
        def per_token(j, carry):
            for r in range(PEER_SLOTS):
                pltpu.make_async_copy(tab_ref.at[pl.ds(idx_ref[gi * grp + j, r], 1), :],
                                      bufs[slot].at[j, pl.ds(r, 1), :], sem_ref.at[slot]).start()
            return carry
        lax.fori_loop(0, grp, per_token, 0)

    def wait_group(slot):
        pltpu.make_async_copy(bufs[slot], bufs[slot], sem_ref.at[slot]).wait()

    lane_blk = lax.broadcasted_iota(jnp.int32, (grp, rows), 1) // PEER_SLOTS
    own = lane_blk == lax.broadcasted_iota(jnp.int32, (grp, rows), 0)

    def compute_group(gi, slot):
        t0 = pl.multiple_of(gi * grp, grp)
        xg = x_ref[pl.ds(t0, grp), :].astype(bf)
        u = bufs[slot][:, :, 0:D_MODEL].reshape(rows, D_MODEL).astype(bf)
        act_all = lax.dot_general(xg, u, _NT, preferred_element_type=jnp.float32)
        act_all = jnp.where(own, act_all, 0.0)
        act = act_all[:, 0:PEER_SLOTS]
        for j in range(1, grp):
            act = act + act_all[:, j * PEER_SLOTS:(j + 1) * PEER_SLOTS]
        gelu = 0.5 * act * (1.0 + lax.erf(act * (2.0 ** -0.5)))
        w = gate_ref[pl.ds(t0, grp), :] * gelu
        w_all = jnp.where(own, jnp.concatenate([w] * grp, axis=1), 0.0).astype(bf)
        v = bufs[slot][:, :, D_MODEL:2 * D_MODEL].reshape(rows, D_MODEL).astype(bf)
        y_ref[pl.ds(t0, grp), :] = jnp.dot(w_all, v, preferred_element_type=jnp.float32)

    def run_group(gi, slot, prefetch):
        if prefetch:
            start_group(gi + 1, 1 - slot)
        wait_group(slot)
        compute_group(gi, slot)

    start_group(0, 0)

    def pair(gp, carry):
        run_group(2 * gp, 0, True)
        run_group(2 * gp + 1, 1, True)
        return carry

    lax.fori_loop(0, ngroups // 2 - 1, pair, 0)
    run_group(ngroups - 2, 0, True)
    run_group(ngroups - 1, 1, False)
    y_ref[...] = _layer_norm(DN_ALPHA * x_ref[...] + y_ref[...], g_ref[...], b_ref[...])


def _peer(idx, x, gate, g, b, table):
    t = x.shape[0]
    tm = PEER_TILE
    row = lambda w: pl.BlockSpec((tm, w), lambda i: (i, 0))
    buf = pltpu.VMEM((PEER_GROUP, PEER_SLOTS, 2 * D_MODEL), jnp.float32)
    return pl.pallas_call(
        _peer_kernel,
        grid=(t // tm,),
        in_specs=[pl.BlockSpec((tm, PEER_SLOTS), lambda i: (i, 0), memory_space=pltpu.SMEM),
                  row(D_MODEL), row(PEER_SLOTS), _full(g.shape), _full(b.shape),
                  pl.BlockSpec(memory_space=pl.ANY)],
        out_specs=row(D_MODEL),
        out_shape=jax.ShapeDtypeStruct((t, D_MODEL), jnp.float32),
        scratch_shapes=[buf, buf, pltpu.SemaphoreType.DMA((2,))],
        compiler_params=_cparams(("arbitrary",)),
        name="peer",
    )(idx, x, gate, g, b, table)


def _rope_tables(seq_lens):
    half = 32
    inv = jnp.power(ROPE_THETA, -jnp.arange(half, dtype=jnp.float32) / half)

    def table(p):
        ang = p.astype(jnp.float32)[:, None] * inv[None, :]
        c, s = jnp.cos(ang), jnp.sin(ang)
        return jnp.concatenate([c, c], axis=-1), jnp.concatenate([-s, s], axis=-1)

    parts = []
    for s in seq_lens:
        pos = jnp.arange(s, dtype=jnp.int32)
        (cr, sr), (cc, sc), (cp, sp) = table(pos // GRID_W), table(pos % GRID_W), table(pos)
        parts.append((jnp.concatenate([cr, cc], -1), jnp.concatenate([sr, sc], -1),
                      jnp.concatenate([cp, cp], -1), jnp.concatenate([sp, sp], -1)))
    return tuple(jnp.concatenate([p[i] for p in parts], axis=0) for i in range(4))


def _layer(x, seqs, tabs, lambda_init, w_in, a_qnorm_g, a_knorm_g, b_qnorm_g, b_kvnorm_g, b_w_uq, b_w_ukv,
           c_lambda_q1, c_lambda_k1, c_lambda_q2, c_lambda_k2, c_subln_g, w_o, ln1_g, ln1_b,
           peer_w_query, peer_sub_keys, peer_u, peer_v, ln2_g, ln2_b):
    bf = jnp.bfloat16
    row = lambda a: a.reshape(1, -1)
    w_in_p = jnp.concatenate([w_in[:, :_B_KR + B_ROPE], jnp.zeros((D_MODEL, LANES - B_ROPE), w_in.dtype),
                              w_in[:, _B_KR + B_ROPE:]], axis=1).astype(bf)
    wuq = b_w_uq.reshape(B_Q_RANK, B_HEADS, B_NOPE + B_ROPE)
    wuq = jnp.pad(wuq, ((0, 0), (0, 0), (0, 2 * LANES - B_NOPE - B_ROPE))).reshape(B_Q_RANK, -1).astype(bf)
    wukv = b_w_ukv.reshape(B_KV_RANK, B_HEADS, B_NOPE + B_V)
    wuk = wukv[:, :, :B_NOPE].reshape(B_KV_RANK, -1).astype(bf)
    wuv = wukv[:, :, B_NOPE:].reshape(B_KV_RANK, -1).astype(bf)
    lamv = jnp.stack([c_lambda_q1, c_lambda_k1, c_lambda_q2, c_lambda_k2], axis=0)
    table = jnp.concatenate([peer_u, peer_v], axis=1)

    qa, ka, va, qb, kb, vb, qc, kc, vc = _inproj(
        x, w_in_p, tabs, row(a_qnorm_g), row(a_knorm_g), row(b_qnorm_g), row(b_kvnorm_g), wuq, wuk, wuv)
    oa = _attend(qa, ka, va, seqs, group=A_HEADS // A_KV_HEADS, dq=HEAD_DIM, out_dtype=bf)
    ob = _attend(qb, kb, vb, seqs, group=1, dq=2 * LANES, out_dtype=bf)
    oc = _attend(qc, kc, vc, seqs, group=2, dq=LANES, out_dtype=jnp.float32)
    x = _outproj(oa, ob, oc, x, w_o.astype(bf), lamv, row(c_subln_g), row(ln1_g), row(ln1_b), lambda_init)
    idx, gate = _route(x, peer_w_query.astype(bf), peer_sub_keys.astype(bf))
    return _peer(idx, x, gate, row(ln2_g), row(ln2_b), table)


def kernel(x_prompt, x_sample, w_in, a_qnorm_g, a_knorm_g, b_qnorm_g, b_kvnorm_g, b_w_uq, b_w_ukv,
           c_lambda_q1, c_lambda_k1, c_lambda_q2, c_lambda_k2, c_subln_g, w_o, ln1_g, ln1_b,
           peer_w_query, peer_sub_keys, peer_u, peer_v, ln2_g, ln2_b):
    nb_p, s_p, _ = x_prompt.shape
    nb_s, s_s, _ = x_sample.shape
    t_p = nb_p * s_p
    x = jnp.concatenate([x_prompt.reshape(t_p, D_MODEL), x_sample.reshape(nb_s * s_s, D_MODEL)], axis=0)
    seqs = ((0, nb_p, s_p), (t_p, nb_s, s_s))
    tabs = _rope_tables([s_p] * nb_p + [s_s] * nb_s)
    params = (w_in, a_qnorm_g, a_knorm_g, b_qnorm_g, b_kvnorm_g, b_w_uq, b_w_ukv, c_lambda_q1, c_lambda_k1,
              c_lambda_q2, c_lambda_k2, c_subln_g, w_o, ln1_g, ln1_b, peer_w_query, peer_sub_keys, peer_u,
              peer_v, ln2_g, ln2_b)
    for l in range(DEPTH):
        x = _layer(x, seqs, tabs, 0.8 - 0.6 * math.exp(-0.3 * l), *(p[l] for p in params))

    y_prompt = x[:t_p].reshape(nb_p, s_p, D_MODEL)
    y_sample = x[t_p:].reshape(nb_s, s_s, D_MODEL)
    return (y_prompt, y_sample)
```

```python
import functools
import math

import jax
import jax.numpy as jnp
from jax import lax
from jax.experimental import pallas as pl
from jax.experimental.pallas import tpu as pltpu

D_MODEL = 2048
DEPTH = 4
GRID_W = 64
ROPE_THETA = 10000.0
EPS = 1e-5
HEAD_DIM = 128
A_HEADS, A_KV_HEADS = 6, 2
B_HEADS, B_Q_RANK, B_KV_RANK, B_NOPE, B_ROPE, B_V = 6, 384, 256, 128, 64, 128
C_HEADS, C_QK = 4, 64
C_V = 2 * C_QK
PEER_HEADS, PEER_NKEYS, PEER_TOPK, PEER_QDIM = 8, 128, 16, 256
PEER_SLOTS = PEER_HEADS * PEER_TOPK
DN_ALPHA = (2 * DEPTH) ** 0.25

LANES = 128
SUBLANES = 8
VMEM_LIMIT = 56 * 1024 * 1024

TOK_TILE = 256
ATT_TQ = 512
ATT_TK = 1024
ATT_QSPLIT = 2
PEER_TILE = 256
PEER_GROUP = SUBLANES
VT_ROWS = LANES + 16

_A_Q, _A_K, _A_V = 0, 768, 1024
_B_CQ, _B_CKV, _B_KR = 1280, 1664, 1920
_C_Q, _C_K, _C_V_OFF = 2048, 2560, 3072
IN_WIDTH_PAD = 3584

_NT = (((1,), (1,)), ((), ()))

_LOG2E = math.log2(math.e)
QSCALE_A = HEAD_DIM ** -0.5 * _LOG2E
QSCALE_B = (B_NOPE + B_ROPE) ** -0.5 * _LOG2E
QSCALE_C = C_QK ** -0.5 * _LOG2E


def _cparams(sem, flags=None):
    return pltpu.CompilerParams(dimension_semantics=sem, vmem_limit_bytes=VMEM_LIMIT, flags=flags)


def _full(shape):
    return pl.BlockSpec(shape, lambda *_: (0,) * len(shape))


def _rms(x, g):
    return x * lax.rsqrt(jnp.mean(x * x, axis=-1, keepdims=True) + EPS) * g


def _layer_norm(y, g, b):
    mu = jnp.mean(y, axis=-1, keepdims=True)
    yc = y - mu
    var = jnp.mean(yc * yc, axis=-1, keepdims=True)
    return yc * lax.rsqrt(var + EPS) * g + b


def _store_vt(vt_ref, v):
    vt = v.T
    ones = jnp.ones((VT_ROWS - LANES, v.shape[0]), vt_ref.dtype)
    for h in range(v.shape[1] // LANES):
        vt_ref[h * VT_ROWS:h * VT_ROWS + LANES, :] = vt[h * LANES:(h + 1) * LANES].astype(vt_ref.dtype)
        vt_ref[h * VT_ROWS + LANES:(h + 1) * VT_ROWS, :] = ones


def _rope64(x, cos, sin_signed):
    lane = lax.broadcasted_iota(jnp.int32, x.shape, 1)
    first = (lane % 64) < 32
    partner = jnp.where(first, pltpu.roll(x, 96, 1), pltpu.roll(x, 32, 1))
    return x * cos + partner * sin_signed


def _inproj_kernel(x_ref, w_ref, cosa_ref, sina_ref, cosp_ref, sinp_ref,
                   aq_g_ref, ak_g_ref, bq_g_ref, bkv_g_ref, wuq_ref, wuk_ref, wuv_ref,
                   qa_ref, ka_ref, va_ref, qb_ref, kb_ref, vb_ref, qc_ref, kc_ref, vc_ref):
    bf = jnp.bfloat16
    xb = x_ref[...].astype(bf)

    def proj(lo, hi):
        return jnp.dot(xb, w_ref[:, lo:hi], preferred_element_type=jnp.float32)

    cosa, sina = cosa_ref[...], sina_ref[...]
    cosp, sinp = cosp_ref[...], sinp_ref[...]

    hq = proj(_A_Q, _A_K)
    for h in range(A_HEADS):
        sl = slice(h * HEAD_DIM, (h + 1) * HEAD_DIM)
        qa_ref[:, sl] = (_rope64(_rms(hq[:, sl], aq_g_ref[...]), cosa, sina) * QSCALE_A).astype(bf)
    hk = proj(_A_K, _A_V)
    for h in range(A_KV_HEADS):
        sl = slice(h * HEAD_DIM, (h + 1) * HEAD_DIM)
        ka_ref[:, sl] = _rope64(_rms(hk[:, sl], ak_g_ref[...]), cosa, sina).astype(bf)
    _store_vt(va_ref, proj(_A_V, _B_CQ))

    cq = _rms(proj(_B_CQ, _B_CKV), bq_g_ref[...]).astype(bf)
    qh = jnp.dot(cq, wuq_ref[...], preferred_element_type=jnp.float32) * QSCALE_B
    ckv = _rms(proj(_B_CKV, _B_KR), bkv_g_ref[...]).astype(bf)
    k_nope = jnp.dot(ckv, wuk_ref[...], preferred_element_type=jnp.float32)
    _store_vt(vb_ref, jnp.dot(ckv, wuv_ref[...], preferred_element_type=jnp.float32))
    k_pe = _rope64(proj(_B_KR, _C_Q), cosp, sinp).astype(bf)
    for h in range(B_HEADS):
        lo = h * 2 * LANES
        qb_ref[:, lo:lo + LANES] = qh[:, lo:lo + LANES].astype(bf)
        qb_ref[:, lo + LANES:lo + 2 * LANES] = _rope64(qh[:, lo + LANES:lo + 2 * LANES], cosp, sinp).astype(bf)
        kb_ref[:, lo:lo + LANES] = k_nope[:, h * LANES:(h + 1) * LANES].astype(bf)
        kb_ref[:, lo + LANES:lo + 2 * LANES] = k_pe

    hq = proj(_C_Q, _C_K)
    lane = lax.broadcasted_iota(jnp.int32, (x_ref.shape[0], LANES), 1)
    for h in range(C_HEADS):
        sl = slice(h * LANES, (h + 1) * LANES)
        qr = _rope64(hq[:, sl], cosp, sinp) * QSCALE_C
        qc_ref[:, (2 * h) * LANES:(2 * h + 1) * LANES] = jnp.where(lane < C_QK, qr, 0.0).astype(bf)
        qc_ref[:, (2 * h + 1) * LANES:(2 * h + 2) * LANES] = jnp.where(lane < C_QK, 0.0, qr).astype(bf)
    hk = proj(_C_K, _C_V_OFF)
    for h in range(C_HEADS):
        sl = slice(h * LANES, (h + 1) * LANES)
        kc_ref[:, sl] = _rope64(hk[:, sl], cosp, sinp).astype(bf)
    _store_vt(vc_ref, proj(_C_V_OFF, IN_WIDTH_PAD))


def _inproj(x, w_in, tabs, aq_g, ak_g, bq_g, bkv_g, wuq, wuk, wuv):
    t = x.shape[0]
    tm = TOK_TILE
    row = lambda w: pl.BlockSpec((tm, w), lambda i: (i, 0))
    col = lambda w: pl.BlockSpec((w, tm), lambda i: (0, i))
    vt_rows = lambda heads: heads * VT_ROWS
    widths = (768, 256, vt_rows(A_KV_HEADS), 1536, 1536, vt_rows(B_HEADS), 1024, 512, vt_rows(C_HEADS))
    is_v = (False, False, True) * 3
    return pl.pallas_call(
        _inproj_kernel,
        grid=(t // tm,),
        in_specs=[row(D_MODEL), _full(w_in.shape), row(LANES), row(LANES), row(LANES), row(LANES),
                  _full(aq_g.shape), _full(ak_g.shape), _full(bq_g.shape), _full(bkv_g.shape),
                  _full(wuq.shape), _full(wuk.shape), _full(wuv.shape)],
        out_specs=[col(w) if tr else row(w) for w, tr in zip(widths, is_v)],
        out_shape=[jax.ShapeDtypeStruct((w, t) if tr else (t, w), jnp.bfloat16) for w, tr in zip(widths, is_v)],
        compiler_params=_cparams(("parallel",)),
        name="inproj",
    )(x, w_in, *tabs, aq_g, ak_g, bq_g, bkv_g, wuq, wuk, wuv)


def _flash_kernel(q_ref, k_ref, vt_ref, o_ref, m_ref, l_ref, acc_ref, *, kv_heads, group, dq):
    ki = pl.program_id(3)

    @pl.when(ki == 0)
    def _():
        m_ref[...] = jnp.full(m_ref.shape, -jnp.inf, jnp.float32)
        l_ref[...] = jnp.zeros(l_ref.shape, jnp.float32)
        acc_ref[...] = jnp.zeros(acc_ref.shape, jnp.float32)

    tqs = q_ref.shape[0] // ATT_QSPLIT
    chains = [(kv, kv * group + g, c * tqs)
              for kv in range(kv_heads) for g in range(group) for c in range(ATT_QSPLIT)]
    scores = [lax.dot_general(k_ref[:, kv * dq:(kv + 1) * dq], q_ref[c0:c0 + tqs, n * dq:(n + 1) * dq], _NT,
                              preferred_element_type=jnp.float32) for kv, n, c0 in chains]
    for (kv, n, c0), st in zip(chains, scores):
        cols = slice(c0, c0 + tqs)
        vt = vt_ref[kv * VT_ROWS:(kv + 1) * VT_ROWS, :]
        m_prev = m_ref[n, :, cols]
        m_new = jnp.maximum(m_prev, jnp.max(st, axis=0, keepdims=True))
        alpha = jnp.exp2(m_prev - m_new)
        p = jnp.exp2(st - m_new).astype(vt.dtype)
        pv = jnp.dot(vt, p, preferred_element_type=jnp.float32)
        l_ref[n, :, cols] = alpha * l_ref[n, :, cols] + pv[LANES:LANES + 1]
        acc_ref[n, :, cols] = alpha * acc_ref[n, :, cols] + pv[0:LANES]
        m_ref[n, :, cols] = m_new

    @pl.when(ki == pl.num_programs(3) - 1)
    def _():
        for n in range(kv_heads * group):
            o_ref[:, n * LANES:(n + 1) * LANES] = (acc_ref[n] / l_ref[n]).T.astype(o_ref.dtype)


def _flash(q, k, vt, *, row0, nseq, seq, kv_heads, group, dq, out_dtype):
    hsteps = k.shape[1] // (dq * kv_heads)
    nh = kv_heads * group
    tq, tk = ATT_TQ, ATT_TK
    nq, nk = seq // tq, seq // tk
    q0, k0 = row0 // tq, row0 // tk
    qmap = lambda s, h, qi, ki: (q0 + s * nq + qi, h)
    kmap = lambda s, h, qi, ki: (k0 + s * nk + ki, h)
    vmap = lambda s, h, qi, ki: (h, k0 + s * nk + ki)
    omap = lambda s, h, qi, ki: (s * nq + qi, h)
    return pl.pallas_call(
        functools.partial(_flash_kernel, kv_heads=kv_heads, group=group, dq=dq),
        grid=(nseq, hsteps, nq, nk),
        in_specs=[pl.BlockSpec((tq, nh * dq), qmap), pl.BlockSpec((tk, kv_heads * dq), kmap),
                  pl.BlockSpec((kv_heads * VT_ROWS, tk), vmap)],
        out_specs=pl.BlockSpec((tq, nh * LANES), omap),
        out_shape=jax.ShapeDtypeStruct((nseq * seq, hsteps * nh * LANES), out_dtype),
        scratch_shapes=[pltpu.VMEM((nh, 1, tq), jnp.float32), pltpu.VMEM((nh, 1, tq), jnp.float32),
                        pltpu.VMEM((nh, LANES, tq), jnp.float32)],
        compiler_params=_cparams(("parallel", "parallel", "parallel", "arbitrary")),
        name="flash",
    )(q, k, vt)


def _attend(q, k, v, seqs, **kw):
    outs = [_flash(q, k, v, row0=row0, nseq=nseq, seq=seq, **kw) for row0, nseq, seq in seqs]
    return jnp.concatenate(outs, axis=0)


def _outproj_kernel(oa_ref, ob_ref, oc_ref, x_ref, wo_ref, lamv_ref, subln_ref, g_ref, b_ref, y_ref,
                    *, lambda_init):
    bf = jnp.bfloat16
    lv = lamv_ref[...]
    lam = (jnp.exp(jnp.sum(lv[0:1] * lv[1:2], axis=-1, keepdims=True))
           - jnp.exp(jnp.sum(lv[2:3] * lv[3:4], axis=-1, keepdims=True)) + lambda_init)
    na, nb = A_HEADS * HEAD_DIM, B_HEADS * B_V
    mix = jnp.dot(oa_ref[...], wo_ref[0:na, :], preferred_element_type=jnp.float32)
    mix += jnp.dot(ob_ref[...], wo_ref[na:na + nb, :], preferred_element_type=jnp.float32)
    for h in range(C_HEADS):
        o1 = oc_ref[:, (2 * h) * LANES:(2 * h + 1) * LANES]
        o2 = oc_ref[:, (2 * h + 1) * LANES:(2 * h + 2) * LANES]
        c = (_rms(o1 - lam * o2, subln_ref[...]) * (1.0 - lambda_init)).astype(bf)
        lo = na + nb + h * C_V
        mix += jnp.dot(c, wo_ref[lo:lo + C_V, :], preferred_element_type=jnp.float32)
    y_ref[...] = _layer_norm(DN_ALPHA * x_ref[...] + mix, g_ref[...], b_ref[...])


def _outproj(oa, ob, oc, x, wo, lamv, subln, g, b, lambda_init):
    t = x.shape[0]
    tm = TOK_TILE
    row = lambda w: pl.BlockSpec((tm, w), lambda i: (i, 0))
    return pl.pallas_call(
        functools.partial(_outproj_kernel, lambda_init=lambda_init),
        grid=(t // tm,),
        in_specs=[row(oa.shape[1]), row(ob.shape[1]), row(oc.shape[1]), row(D_MODEL), _full(wo.shape),
                  _full(lamv.shape), _full(subln.shape), _full(g.shape), _full(b.shape)],
        out_specs=row(D_MODEL),
        out_shape=jax.ShapeDtypeStruct((t, D_MODEL), jnp.float32),
        compiler_params=_cparams(("parallel",)),
        name="outproj",
    )(oa, ob, oc, x, wo, lamv, subln, g, b)


def _topk_rows(s, payload, k):
    n = s.shape[0]
    iota = lax.broadcasted_iota(jnp.int32, s.shape, 0)
    vals, idxs, pays = [], [], []
    for _ in range(k):
        m = jnp.max(s, axis=0, keepdims=True)
        i = jnp.min(jnp.where(s == m, iota, n), axis=0, keepdims=True)
        hit = iota == i
        vals.append(m)
        idxs.append(i)
        if payload is not None:
            pays.append(jnp.sum(jnp.where(hit, payload, 0), axis=0, keepdims=True))
        s = jnp.where(hit, -jnp.inf, s)
    cat = lambda xs: jnp.concatenate(xs, axis=0)
    return cat(vals), cat(idxs), (cat(pays) if payload is not None else None)


def _pair_candidates(v1, i1, v2, i2):
    k = PEER_TOPK
    sub = lax.broadcasted_iota(jnp.int32, (SUBLANES, v1.shape[1]), 0)
    sums = [v1[0:1] + v2]
    ids = [i1[0:1] * PEER_NKEYS + i2]
    for a in range(1, SUBLANES):
        nb = k // (a + 1)
        sums.append(jnp.where(sub < nb, v1[a:a + 1] + v2[0:SUBLANES], -jnp.inf))
        ids.append(i1[a:a + 1] * PEER_NKEYS + i2[0:SUBLANES])
    sums.append(v1[SUBLANES:k] + v2[0:1])
    ids.append(i1[SUBLANES:k] * PEER_NKEYS + i2[0:1])
    return jnp.concatenate(sums, axis=0), jnp.concatenate(ids, axis=0)


def _route_kernel(x_ref, wq_ref, keys_ref, idx_ref, gate_ref):
    bf = jnp.bfloat16
    q = jnp.dot(x_ref[...].astype(bf), wq_ref[...], preferred_element_type=jnp.float32).astype(bf)
    half = PEER_QDIM // 2
    ids, gates = [], []
    for h in range(PEER_HEADS):
        tops = []
        for m in range(2):
            lo = (h * 2 + m) * half
            s = lax.dot_general(keys_ref[m], q[:, lo:lo + half], _NT, preferred_element_type=jnp.float32)
            v, i, _ = _topk_rows(s, None, PEER_TOPK)
            tops.append((v, i))
        cand, cand_id = _pair_candidates(*tops[0], *tops[1])
        top_s, _, top_id = _topk_rows(cand, cand_id, PEER_TOPK)
        e = jnp.exp(top_s - top_s[0:1])
        gates.append(e / jnp.sum(e, axis=0, keepdims=True))
        ids.append(top_id)
    ids = jnp.concatenate(ids, axis=0)
    gates = jnp.concatenate(gates, axis=0)
    idx_ref[...] = ids.astype(jnp.float32).T.astype(jnp.int32)
    gate_ref[...] = gates.T


def _route(x, wq, keys):
    t = x.shape[0]
    tm = TOK_TILE
    row = lambda w: pl.BlockSpec((tm, w), lambda i: (i, 0))
    return pl.pallas_call(
        _route_kernel,
        grid=(t // tm,),
        in_specs=[row(D_MODEL), _full(wq.shape), _full(keys.shape)],
        out_specs=[row(PEER_SLOTS), row(PEER_SLOTS)],
        out_shape=[jax.ShapeDtypeStruct((t, PEER_SLOTS), jnp.int32),
                   jax.ShapeDtypeStruct((t, PEER_SLOTS), jnp.float32)],
        compiler_params=_cparams(("parallel",)),
        name="route",
    )(x, wq, keys)


def _peer_kernel(idx_ref, x_ref, gate_ref, g_ref, b_ref, tab_ref, y_ref, buf0_ref, buf1_ref, sem_ref):
    bf = jnp.bfloat16
    grp = PEER_GROUP
    rows = grp * PEER_SLOTS
    ngroups = x_ref.shape[0] // grp
    bufs = (buf0_ref, buf1_ref)

    def start_group(gi, slot):
        for j in range(grp):
            for r in range(PEER_SLOTS):
                pltpu.make_async_copy(tab_ref.at[pl.ds(idx_ref[gi * grp + j, r], 1), :],
                                      bufs[slot].at[j, pl.ds(r, 1), :], sem_ref.at[slot]).start()

    def wait_group(slot):
        pltpu.make_async_copy(bufs[slot], bufs[slot], sem_ref.at[slot]).wait()

    lane_blk = lax.broadcasted_iota(jnp.int32, (grp, rows), 1) // PEER_SLOTS
    own = lane_blk == lax.broadcasted_iota(jnp.int32, (grp, rows), 0)

    def compute_group(gi, slot):
        t0 = pl.multiple_of(gi * grp, grp)
        xg = x_ref[pl.ds(t0, grp), :].astype(bf)
        u = bufs[slot][:, :, 0:D_MODEL].reshape(rows, D_MODEL).astype(bf)
        act_all = lax.dot_general(xg, u, _NT, preferred_element_type=jnp.float32)
        act_all = jnp.where(own, act_all, 0.0)
        act = act_all[:, 0:PEER_SLOTS]
        for j in range(1, grp):
            act = act + act_all[:, j * PEER_SLOTS:(j + 1) * PEER_SLOTS]
        gelu = 0.5 * act * (1.0 + lax.erf(act * (2.0 ** -0.5)))
        w = gate_ref[pl.ds(t0, grp), :] * gelu
        w_all = jnp.where(own, jnp.concatenate([w] * grp, axis=1), 0.0).astype(bf)
        v = bufs[slot][:, :, D_MODEL:2 * D_MODEL].reshape(rows, D_MODEL).astype(bf)
        y_ref[pl.ds(t0, grp), :] = jnp.dot(w_all, v, preferred_element_type=jnp.float32)

    start_group(0, 0)
    npairs = ngroups // 2

    def pair(gp, carry):
        start_group(2 * gp + 1, 1)
        wait_group(0)
        compute_group(2 * gp, 0)

        @pl.when(gp + 1 < npairs)
        def _():
            start_group(2 * gp + 2, 0)

        wait_group(1)
        compute_group(2 * gp + 1, 1)
        return carry

    lax.fori_loop(0, npairs, pair, 0)
    y_ref[...] = _layer_norm(DN_ALPHA * x_ref[...] + y_ref[...], g_ref[...], b_ref[...])


def _peer(idx, x, gate, g, b, table):
    t = x.shape[0]
    tm = PEER_TILE
    row = lambda w: pl.BlockSpec((tm, w), lambda i: (i, 0))
    buf = pltpu.VMEM((PEER_GROUP, PEER_SLOTS, 2 * D_MODEL), jnp.float32)
    return pl.pallas_call(
        _peer_kernel,
        grid=(t // tm,),
        in_specs=[pl.BlockSpec((tm, PEER_SLOTS), lambda i: (i, 0), memory_space=pltpu.SMEM),
                  row(D_MODEL), row(PEER_SLOTS), _full(g.shape), _full(b.shape),
                  pl.BlockSpec(memory_space=pl.ANY)],
        out_specs=row(D_MODEL),
        out_shape=jax.ShapeDtypeStruct((t, D_MODEL), jnp.float32),
        scratch_shapes=[buf, buf, pltpu.SemaphoreType.DMA((2,))],
        compiler_params=_cparams(("arbitrary",)),
        name="peer",
    )(idx, x, gate, g, b, table)


def _rope_tables(seq_lens):
    half = 32
    inv = jnp.power(ROPE_THETA, -jnp.arange(half, dtype=jnp.float32) / half)

    def table(p):
        ang = p.astype(jnp.float32)[:, None] * inv[None, :]
        c, s = jnp.cos(ang), jnp.sin(ang)
        return jnp.concatenate([c, c], axis=-1), jnp.concatenate([-s, s], axis=-1)

    parts = []
    for s in seq_lens:
        pos = jnp.arange(s, dtype=jnp.int32)
        (cr, sr), (cc, sc), (cp, sp) = table(pos // GRID_W), table(pos % GRID_W), table(pos)
        parts.append((jnp.concatenate([cr, cc], -1), jnp.concatenate([sr, sc], -1),
                      jnp.concatenate([cp, cp], -1), jnp.concatenate([sp, sp], -1)))
    return tuple(jnp.concatenate([p[i] for p in parts], axis=0) for i in range(4))


def _layer(x, seqs, tabs, lambda_init, w_in, a_qnorm_g, a_knorm_g, b_qnorm_g, b_kvnorm_g, b_w_uq, b_w_ukv,
           c_lambda_q1, c_lambda_k1, c_lambda_q2, c_lambda_k2, c_subln_g, w_o, ln1_g, ln1_b,
           peer_w_query, peer_sub_keys, peer_u, peer_v, ln2_g, ln2_b):
    bf = jnp.bfloat16
    row = lambda a: a.reshape(1, -1)
    w_in_p = jnp.concatenate([w_in[:, :_B_KR + B_ROPE], jnp.zeros((D_MODEL, LANES - B_ROPE), w_in.dtype),
                              w_in[:, _B_KR + B_ROPE:]], axis=1).astype(bf)
    wuq = b_w_uq.reshape(B_Q_RANK, B_HEADS, B_NOPE + B_ROPE)
    wuq = jnp.pad(wuq, ((0, 0), (0, 0), (0, 2 * LANES - B_NOPE - B_ROPE))).reshape(B_Q_RANK, -1).astype(bf)
    wukv = b_w_ukv.reshape(B_KV_RANK, B_HEADS, B_NOPE + B_V)
    wuk = wukv[:, :, :B_NOPE].reshape(B_KV_RANK, -1).astype(bf)
    wuv = wukv[:, :, B_NOPE:].reshape(B_KV_RANK, -1).astype(bf)
    lamv = jnp.stack([c_lambda_q1, c_lambda_k1, c_lambda_q2, c_lambda_k2], axis=0)
    table = jnp.concatenate([peer_u, peer_v], axis=1)

    qa, ka, va, qb, kb, vb, qc, kc, vc = _inproj(
        x, w_in_p, tabs, row(a_qnorm_g), row(a_knorm_g), row(b_qnorm_g), row(b_kvnorm_g), wuq, wuk, wuv)
    oa = _attend(qa, ka, va, seqs, kv_heads=2, group=A_HEADS // A_KV_HEADS, dq=HEAD_DIM, out_dtype=bf)
    ob = _attend(qb, kb, vb, seqs, kv_heads=3, group=1, dq=2 * LANES, out_dtype=bf)
    oc = _attend(qc, kc, vc, seqs, kv_heads=2, group=2, dq=LANES, out_dtype=jnp.float32)
    x = _outproj(oa, ob, oc, x, w_o.astype(bf), lamv, row(c_subln_g), row(ln1_g), row(ln1_b), lambda_init)
    idx, gate = _route(x, peer_w_query.astype(bf), peer_sub_keys.astype(bf))
    return _peer(idx, x, gate, row(ln2_g), row(ln2_b), table)


def kernel(x_prompt, x_sample, w_in, a_qnorm_g, a_knorm_g, b_qnorm_g, b_kvnorm_g, b_w_uq, b_w_ukv,
           c_lambda_q1, c_lambda_k1, c_lambda_q2, c_lambda_k2, c_subln_g, w_o, ln1_g, ln1_b,
           peer_w_query, peer_sub_keys, peer_u, peer_v, ln2_g, ln2_b):
    nb_p, s_p, _ = x_prompt.shape
    nb_s, s_s, _ = x_sample.shape
    t_p = nb_p * s_p
    x = jnp.concatenate([x_prompt.reshape(t_p, D_MODEL), x_sample.reshape(nb_s * s_s, D_MODEL)], axis=0)
    seqs = ((0, nb_p, s_p), (t_p, nb_s, s_s))
    tabs = _rope_tables([s_p] * nb_p + [s_s] * nb_s)
    params = (w_in, a_qnorm_g, a_knorm_g, b_qnorm_g, b_kvnorm_g, b_w_uq, b_w_ukv, c_lambda_q1, c_lambda_k1,
              c_lambda_q2, c_lambda_k2, c_subln_g, w_o, ln1_g, ln1_b, peer_w_query, peer_sub_keys, peer_u,
              peer_v, ln2_g, ln2_b)
    for l in range(DEPTH):
        x = _layer(x, seqs, tabs, 0.8 - 0.6 * math.exp(-0.3 * l), *(p[l] for p in params))

    y_prompt = x[:t_p].reshape(nb_p, s_p, D_MODEL)
    y_sample = x[t_p:].reshape(nb_s, s_s, D_MODEL)
    return (y_prompt, y_sample)
```

```python
import functools
import math

import jax
import jax.numpy as jnp
from jax import lax
from jax.experimental import pallas as pl
from jax.experimental.pallas import tpu as pltpu

D_MODEL = 2048
DEPTH = 4
GRID_W = 64
ROPE_THETA = 10000.0
EPS = 1e-5
HEAD_DIM = 128
A_HEADS, A_KV_HEADS = 6, 2
B_HEADS, B_Q_RANK, B_KV_RANK, B_NOPE, B_ROPE, B_V = 6, 384, 256, 128, 64, 128
C_HEADS, C_QK = 4, 64
C_V = 2 * C_QK
PEER_HEADS, PEER_NKEYS, PEER_TOPK, PEER_QDIM = 8, 128, 16, 256
PEER_SLOTS = PEER_HEADS * PEER_TOPK
DN_ALPHA = (2 * DEPTH) ** 0.25

LANES = 128
SUBLANES = 8
VMEM_LIMIT = 56 * 1024 * 1024

TOK_TILE = 256
ATT_TQ = 512
ATT_TK = 1024
ATT_QSPLIT = 2
PEER_TILE = 256
PEER_GROUP = SUBLANES
VT_ROWS = LANES + 16

_A_Q, _A_K, _A_V = 0, 768, 1024
_B_CQ, _B_CKV, _B_KR = 1280, 1664, 1920
_C_Q, _C_K, _C_V_OFF = 2048, 2560, 3072
IN_WIDTH_PAD = 3584

_NT = (((1,), (1,)), ((), ()))

_LOG2E = math.log2(math.e)
QSCALE_A = HEAD_DIM ** -0.5 * _LOG2E
QSCALE_B = (B_NOPE + B_ROPE) ** -0.5 * _LOG2E
QSCALE_C = C_QK ** -0.5 * _LOG2E


def _cparams(sem, flags=None):
    return pltpu.CompilerParams(dimension_semantics=sem, vmem_limit_bytes=VMEM_LIMIT, flags=flags)


def _full(shape):
    return pl.BlockSpec(shape, lambda *_: (0,) * len(shape))


def _rms(x, g):
    return x * lax.rsqrt(jnp.mean(x * x, axis=-1, keepdims=True) + EPS) * g


def _layer_norm(y, g, b):
    mu = jnp.mean(y, axis=-1, keepdims=True)
    yc = y - mu
    var = jnp.mean(yc * yc, axis=-1, keepdims=True)
    return yc * lax.rsqrt(var + EPS) * g + b


def _store_vt(vt_ref, v):
    vt = v.T
    ones = jnp.ones((VT_ROWS - LANES, v.shape[0]), vt_ref.dtype)
    for h in range(v.shape[1] // LANES):
        vt_ref[h * VT_ROWS:h * VT_ROWS + LANES, :] = vt[h * LANES:(h + 1) * LANES].astype(vt_ref.dtype)
        vt_ref[h * VT_ROWS + LANES:(h + 1) * VT_ROWS, :] = ones


def _rope64(x, cos, sin_signed):
    lane = lax.broadcasted_iota(jnp.int32, x.shape, 1)
    first = (lane % 64) < 32
    partner = jnp.where(first, pltpu.roll(x, 96, 1), pltpu.roll(x, 32, 1))
    return x * cos + partner * sin_signed


def _inproj_kernel(x_ref, w_ref, cosa_ref, sina_ref, cosp_ref, sinp_ref,
                   aq_g_ref, ak_g_ref, bq_g_ref, bkv_g_ref, wuq_ref, wuk_ref, wuv_ref,
                   qa_ref, ka_ref, va_ref, qb_ref, kb_ref, vb_ref, qc_ref, kc_ref, vc_ref):
    bf = jnp.bfloat16
    xb = x_ref[...].astype(bf)

    def proj(lo, hi):
        return jnp.dot(xb, w_ref[:, lo:hi], preferred_element_type=jnp.float32)

    cosa, sina = cosa_ref[...], sina_ref[...]
    cosp, sinp = cosp_ref[...], sinp_ref[...]

    hq = proj(_A_Q, _A_K)
    for h in range(A_HEADS):
        sl = slice(h * HEAD_DIM, (h + 1) * HEAD_DIM)
        qa_ref[:, sl] = (_rope64(_rms(hq[:, sl], aq_g_ref[...]), cosa, sina) * QSCALE_A).astype(bf)
    hk = proj(_A_K, _A_V)
    for h in range(A_KV_HEADS):
        sl = slice(h * HEAD_DIM, (h + 1) * HEAD_DIM)
        ka_ref[:, sl] = _rope64(_rms(hk[:, sl], ak_g_ref[...]), cosa, sina).astype(bf)
    _store_vt(va_ref, proj(_A_V, _B_CQ))

    cq = _rms(proj(_B_CQ, _B_CKV), bq_g_ref[...]).astype(bf)
    qh = jnp.dot(cq, wuq_ref[...], preferred_element_type=jnp.float32) * QSCALE_B
    ckv = _rms(proj(_B_CKV, _B_KR), bkv_g_ref[...]).astype(bf)
    k_nope = jnp.dot(ckv, wuk_ref[...], preferred_element_type=jnp.float32)
    _store_vt(vb_ref, jnp.dot(ckv, wuv_ref[...], preferred_element_type=jnp.float32))
    k_pe = _rope64(proj(_B_KR, _C_Q), cosp, sinp).astype(bf)
    for h in range(B_HEADS):
        lo = h * 2 * LANES
        qb_ref[:, lo:lo + LANES] = qh[:, lo:lo + LANES].astype(bf)
        qb_ref[:, lo + LANES:lo + 2 * LANES] = _rope64(qh[:, lo + LANES:lo + 2 * LANES], cosp, sinp).astype(bf)
        kb_ref[:, lo:lo + LANES] = k_nope[:, h * LANES:(h + 1) * LANES].astype(bf)
        kb_ref[:, lo + LANES:lo + 2 * LANES] = k_pe

    hq = proj(_C_Q, _C_K)
    lane = lax.broadcasted_iota(jnp.int32, (x_ref.shape[0], LANES), 1)
    for h in range(C_HEADS):
        sl = slice(h * LANES, (h + 1) * LANES)
        qr = _rope64(hq[:, sl], cosp, sinp) * QSCALE_C
        qc_ref[:, (2 * h) * LANES:(2 * h + 1) * LANES] = jnp.where(lane < C_QK, qr, 0.0).astype(bf)
        qc_ref[:, (2 * h + 1) * LANES:(2 * h + 2) * LANES] = jnp.where(lane < C_QK, 0.0, qr).astype(bf)
    hk = proj(_C_K, _C_V_OFF)
    for h in range(C_HEADS):
        sl = slice(h * LANES, (h + 1) * LANES)
        kc_ref[:, sl] = _rope64(hk[:, sl], cosp, sinp).astype(bf)
    _store_vt(vc_ref, proj(_C_V_OFF, IN_WIDTH_PAD))


def _inproj(x, w_in, tabs, aq_g, ak_g, bq_g, bkv_g, wuq, wuk, wuv):
    t = x.shape[0]
    tm = TOK_TILE
    row = lambda w: pl.BlockSpec((tm, w), lambda i: (i, 0))
    col = lambda w: pl.BlockSpec((w, tm), lambda i: (0, i))
    vt_rows = lambda heads: heads * VT_ROWS
    widths = (768, 256, vt_rows(A_KV_HEADS), 1536, 1536, vt_rows(B_HEADS), 1024, 512, vt_rows(C_HEADS))
    is_v = (False, False, True) * 3
    return pl.pallas_call(
        _inproj_kernel,
        grid=(t // tm,),
        in_specs=[row(D_MODEL), _full(w_in.shape), row(LANES), row(LANES), row(LANES), row(LANES),
                  _full(aq_g.shape), _full(ak_g.shape), _full(bq_g.shape), _full(bkv_g.shape),
                  _full(wuq.shape), _full(wuk.shape), _full(wuv.shape)],
        out_specs=[col(w) if tr else row(w) for w, tr in zip(widths, is_v)],
        out_shape=[jax.ShapeDtypeStruct((w, t) if tr else (t, w), jnp.bfloat16) for w, tr in zip(widths, is_v)],
        compiler_params=_cparams(("parallel",)),
        name="inproj",
    )(x, w_in, *tabs, aq_g, ak_g, bq_g, bkv_g, wuq, wuk, wuv)


def _flash_kernel(q_ref, k_ref, vt_ref, o_ref, m_ref, l_ref, acc_ref, *, kv_heads, group, dq):
    ki = pl.program_id(3)

    @pl.when(ki == 0)
    def _():
        m_ref[...] = jnp.full(m_ref.shape, -jnp.inf, jnp.float32)
        l_ref[...] = jnp.zeros(l_ref.shape, jnp.float32)
        acc_ref[...] = jnp.zeros(acc_ref.shape, jnp.float32)

    tqs = q_ref.shape[0] // ATT_QSPLIT
    chains = [(kv, kv * group + g, c * tqs)
              for kv in range(kv_heads) for g in range(group) for c in range(ATT_QSPLIT)]
    scores = [lax.dot_general(k_ref[:, kv * dq:(kv + 1) * dq], q_ref[c0:c0 + tqs, n * dq:(n + 1) * dq], _NT,
                              preferred_element_type=jnp.float32) for kv, n, c0 in chains]
    for (kv, n, c0), st in zip(chains, scores):
        cols = slice(c0, c0 + tqs)
        vt = vt_ref[kv * VT_ROWS:(kv + 1) * VT_ROWS, :]
        m_prev = m_ref[n, :, cols]
        m_new = jnp.maximum(m_prev, jnp.max(st, axis=0, keepdims=True))
        alpha = jnp.exp2(m_prev - m_new)
        p = jnp.exp2(st - m_new).astype(vt.dtype)
        pv = jnp.dot(vt, p, preferred_element_type=jnp.float32)
        l_ref[n, :, cols] = alpha * l_ref[n, :, cols] + pv[LANES:LANES + 1]
        acc_ref[n, :, cols] = alpha * acc_ref[n, :, cols] + pv[0:LANES]
        m_ref[n, :, cols] = m_new

    @pl.when(ki == pl.num_programs(3) - 1)
    def _():
        for n in range(kv_heads * group):
            o_ref[:, n * LANES:(n + 1) * LANES] = (acc_ref[n] / l_ref[n]).T.astype(o_ref.dtype)


def _flash(q, k, vt, *, row0, nseq, seq, kv_heads, group, dq, out_dtype):
    hsteps = k.shape[1] // (dq * kv_heads)
    nh = kv_heads * group
    tq, tk = ATT_TQ, ATT_TK
    nq, nk = seq // tq, seq // tk
    q0, k0 = row0 // tq, row0 // tk
    qmap = lambda s, h, qi, ki: (q0 + s * nq + qi, h)
    kmap = lambda s, h, qi, ki: (k0 + s * nk + ki, h)
    vmap = lambda s, h, qi, ki: (h, k0 + s * nk + ki)
    omap = lambda s, h, qi, ki: (s * nq + qi, h)
    return pl.pallas_call(
        functools.partial(_flash_kernel, kv_heads=kv_heads, group=group, dq=dq),
        grid=(nseq, hsteps, nq, nk),
        in_specs=[pl.BlockSpec((tq, nh * dq), qmap), pl.BlockSpec((tk, kv_heads * dq), kmap),
                  pl.BlockSpec((kv_heads * VT_ROWS, tk), vmap)],
        out_specs=pl.BlockSpec((tq, nh * LANES), omap),
        out_shape=jax.ShapeDtypeStruct((nseq * seq, hsteps * nh * LANES), out_dtype),
        scratch_shapes=[pltpu.VMEM((nh, 1, tq), jnp.float32), pltpu.VMEM((nh, 1, tq), jnp.float32),
                        pltpu.VMEM((nh, LANES, tq), jnp.float32)],
        compiler_params=_cparams(("parallel", "parallel", "parallel", "arbitrary")),
        name="flash",
    )(q, k, vt)


def _attend(q, k, v, seqs, **kw):
    outs = [_flash(q, k, v, row0=row0, nseq=nseq, seq=seq, **kw) for row0, nseq, seq in seqs]
    return jnp.concatenate(outs, axis=0)


def _outproj_kernel(oa_ref, ob_ref, oc_ref, x_ref, wo_ref, lamv_ref, subln_ref, g_ref, b_ref, y_ref,
                    *, lambda_init):
    bf = jnp.bfloat16
    lv = lamv_ref[...]
    lam = (jnp.exp(jnp.sum(lv[0:1] * lv[1:2], axis=-1, keepdims=True))
           - jnp.exp(jnp.sum(lv[2:3] * lv[3:4], axis=-1, keepdims=True)) + lambda_init)
    na, nb = A_HEADS * HEAD_DIM, B_HEADS * B_V
    mix = jnp.dot(oa_ref[...], wo_ref[0:na, :], preferred_element_type=jnp.float32)
    mix += jnp.dot(ob_ref[...], wo_ref[na:na + nb, :], preferred_element_type=jnp.float32)
    for h in range(C_HEADS):
        o1 = oc_ref[:, (2 * h) * LANES:(2 * h + 1) * LANES]
        o2 = oc_ref[:, (2 * h + 1) * LANES:(2 * h + 2) * LANES]
        c = (_rms(o1 - lam * o2, subln_ref[...]) * (1.0 - lambda_init)).astype(bf)
        lo = na + nb + h * C_V
        mix += jnp.dot(c, wo_ref[lo:lo + C_V, :], preferred_element_type=jnp.float32)
    y_ref[...] = _layer_norm(DN_ALPHA * x_ref[...] + mix, g_ref[...], b_ref[...])


def _outproj(oa, ob, oc, x, wo, lamv, subln, g, b, lambda_init):
    t = x.shape[0]
    tm = TOK_TILE
    row = lambda w: pl.BlockSpec((tm, w), lambda i: (i, 0))
    return pl.pallas_call(
        functools.partial(_outproj_kernel, lambda_init=lambda_init),
        grid=(t // tm,),
        in_specs=[row(oa.shape[1]), row(ob.shape[1]), row(oc.shape[1]), row(D_MODEL), _full(wo.shape),
                  _full(lamv.shape), _full(subln.shape), _full(g.shape), _full(b.shape)],
        out_specs=row(D_MODEL),
        out_shape=jax.ShapeDtypeStruct((t, D_MODEL), jnp.float32),
        compiler_params=_cparams(("parallel",)),
        name="outproj",
    )(oa, ob, oc, x, wo, lamv, subln, g, b)


def _topk_rows(s, payload, k):
    n = s.shape[0]
    iota = lax.broadcasted_iota(jnp.int32, s.shape, 0)
    vals, idxs, pays = [], [], []
    for _ in range(k):
        m = jnp.max(s, axis=0, keepdims=True)
        i = jnp.min(jnp.where(s == m, iota, n), axis=0, keepdims=True)
        hit = iota == i
        vals.append(m)
        idxs.append(i)
        if payload is not None:
            pays.append(jnp.sum(jnp.where(hit, payload, 0), axis=0, keepdims=True))
        s = jnp.where(hit, -jnp.inf, s)
    cat = lambda xs: jnp.concatenate(xs, axis=0)
    return cat(vals), cat(idxs), (cat(pays) if payload is not None else None)


def _pair_candidates(v1, i1, v2, i2):
    k = PEER_TOPK
    sub = lax.broadcasted_iota(jnp.int32, (SUBLANES, v1.shape[1]), 0)
    sums = [v1[0:1] + v2]
    ids = [i1[0:1] * PEER_NKEYS + i2]
    for a in range(1, SUBLANES):
        nb = k // (a + 1)
        sums.append(jnp.where(sub < nb, v1[a:a + 1] + v2[0:SUBLANES], -jnp.inf))
        ids.append(i1[a:a + 1] * PEER_NKEYS + i2[0:SUBLANES])
    sums.append(v1[SUBLANES:k] + v2[0:1])
    ids.append(i1[SUBLANES:k] * PEER_NKEYS + i2[0:1])
    return jnp.concatenate(sums, axis=0), jnp.concatenate(ids, axis=0)


def _route_kernel(x_ref, wq_ref, keys_ref, idx_ref, gate_ref):
    bf = jnp.bfloat16
    q = jnp.dot(x_ref[...].astype(bf), wq_ref[...], preferred_element_type=jnp.float32).astype(bf)
    half = PEER_QDIM // 2
    ids, gates = [], []
    for h in range(PEER_HEADS):
        tops = []
        for m in range(2):
            lo = (h * 2 + m) * half
            s = lax.dot_general(keys_ref[m], q[:, lo:lo + half], _NT, preferred_element_type=jnp.float32)
            v, i, _ = _topk_rows(s, None, PEER_TOPK)
            tops.append((v, i))
        cand, cand_id = _pair_candidates(*tops[0], *tops[1])
        top_s, _, top_id = _topk_rows(cand, cand_id, PEER_TOPK)
        e = jnp.exp(top_s - top_s[0:1])
        gates.append(e / jnp.sum(e, axis=0, keepdims=True))
        ids.append(top_id)
    ids = jnp.concatenate(ids, axis=0)
    gates = jnp.concatenate(gates, axis=0)
    idx_ref[...] = ids.astype(jnp.float32).T.astype(jnp.int32)
    gate_ref[...] = gates.T


def _route(x, wq, keys):
    t = x.shape[0]
    tm = TOK_TILE
    row = lambda w: pl.BlockSpec((tm, w), lambda i: (i, 0))
    return pl.pallas_call(
        _route_kernel,
        grid=(t // tm,),
        in_specs=[row(D_MODEL), _full(wq.shape), _full(keys.shape)],
        out_specs=[row(PEER_SLOTS), row(PEER_SLOTS)],
        out_shape=[jax.ShapeDtypeStruct((t, PEER_SLOTS), jnp.int32),
                   jax.ShapeDtypeStruct((t, PEER_SLOTS), jnp.float32)],
        compiler_params=_cparams(("parallel",)),
        name="route",
    )(x, wq, keys)


def _peer_kernel(idx_ref, x_ref, gate_ref, g_ref, b_ref, tab_ref, y_ref, buf0_ref, buf1_ref, sem_ref):
    bf = jnp.bfloat16
    grp = PEER_GROUP
    rows = grp * PEER_SLOTS
    ngroups = x_ref.shape[0] // grp
    bufs = (buf0_ref, buf1_ref)

    def start_group(gi, slot):
        for j in range(grp):
            for r in range(PEER_SLOTS):
                pltpu.make_async_copy(tab_ref.at[pl.ds(idx_ref[gi * grp + j, r], 1), :],
                                      bufs[slot].at[j, pl.ds(r, 1), :], sem_ref.at[slot]).start(priority=r % 2)

    def wait_group(slot):
        pltpu.make_async_copy(bufs[slot], bufs[slot], sem_ref.at[slot]).wait()

    lane_blk = lax.broadcasted_iota(jnp.int32, (grp, rows), 1) // PEER_SLOTS
    own = lane_blk == lax.broadcasted_iota(jnp.int32, (grp, rows), 0)

    def compute_group(gi, slot):
        t0 = pl.multiple_of(gi * grp, grp)
        xg = x_ref[pl.ds(t0, grp), :].astype(bf)
        u = bufs[slot][:, :, 0:D_MODEL].reshape(rows, D_MODEL).astype(bf)
        act_all = lax.dot_general(xg, u, _NT, preferred_element_type=jnp.float32)
        act_all = jnp.where(own, act_all, 0.0)
        act = act_all[:, 0:PEER_SLOTS]
        for j in range(1, grp):
            act = act + act_all[:, j * PEER_SLOTS:(j + 1) * PEER_SLOTS]
        gelu = 0.5 * act * (1.0 + lax.erf(act * (2.0 ** -0.5)))
        w = gate_ref[pl.ds(t0, grp), :] * gelu
        w_all = jnp.where(own, jnp.concatenate([w] * grp, axis=1), 0.0).astype(bf)
        v = bufs[slot][:, :, D_MODEL:2 * D_MODEL].reshape(rows, D_MODEL).astype(bf)
        y_ref[pl.ds(t0, grp), :] = jnp.dot(w_all, v, preferred_element_type=jnp.float32)

    start_group(0, 0)
    npairs = ngroups // 2

    def pair(gp, carry):
        start_group(2 * gp + 1, 1)
        wait_group(0)
        compute_group(2 * gp, 0)

        @pl.when(gp + 1 < npairs)
        def _():
            start_group(2 * gp + 2, 0)

        wait_group(1)
        compute_group(2 * gp + 1, 1)
        return carry

    lax.fori_loop(0, npairs, pair, 0)
    y_ref[...] = _layer_norm(DN_ALPHA * x_ref[...] + y_ref[...], g_ref[...], b_ref[...])


def _peer(idx, x, gate, g, b, table):
    t = x.shape[0]
    tm = PEER_TILE
    row = lambda w: pl.BlockSpec((tm, w), lambda i: (i, 0))
    buf = pltpu.VMEM((PEER_GROUP, PEER_SLOTS, 2 * D_MODEL), jnp.float32)
    return pl.pallas_call(
        _peer_kernel,
        grid=(t // tm,),
        in_specs=[pl.BlockSpec((tm, PEER_SLOTS), lambda i: (i, 0), memory_space=pltpu.SMEM),
                  row(D_MODEL), row(PEER_SLOTS), _full(g.shape), _full(b.shape),
                  pl.BlockSpec(memory_space=pl.ANY)],
        out_specs=row(D_MODEL),
        out_shape=jax.ShapeDtypeStruct((t, D_MODEL), jnp.float32),
        scratch_shapes=[buf, buf, pltpu.SemaphoreType.DMA((2,))],
        compiler_params=_cparams(("arbitrary",)),
        name="peer",
    )(idx, x, gate, g, b, table)


def _rope_tables(seq_lens):
    half = 32
    inv = jnp.power(ROPE_THETA, -jnp.arange(half, dtype=jnp.float32) / half)

    def table(p):
        ang = p.astype(jnp.float32)[:, None] * inv[None, :]
        c, s = jnp.cos(ang), jnp.sin(ang)
        return jnp.concatenate([c, c], axis=-1), jnp.concatenate([-s, s], axis=-1)

    parts = []
    for s in seq_lens:
        pos = jnp.arange(s, dtype=jnp.int32)
        (cr, sr), (cc, sc), (cp, sp) = table(pos // GRID_W), table(pos % GRID_W), table(pos)
        parts.append((jnp.concatenate([cr, cc], -1), jnp.concatenate([sr, sc], -1),
                      jnp.concatenate([cp, cp], -1), jnp.concatenate([sp, sp], -1)))
    return tuple(jnp.concatenate([p[i] for p in parts], axis=0) for i in range(4))


def _layer(x, seqs, tabs, lambda_init, w_in, a_qnorm_g, a_knorm_g, b_qnorm_g, b_kvnorm_g, b_w_uq, b_w_ukv,
           c_lambda_q1, c_lambda_k1, c_lambda_q2, c_lambda_k2, c_subln_g, w_o, ln1_g, ln1_b,
           peer_w_query, peer_sub_keys, peer_u, peer_v, ln2_g, ln2_b):
    bf = jnp.bfloat16
    row = lambda a: a.reshape(1, -1)
    w_in_p = jnp.concatenate([w_in[:, :_B_KR + B_ROPE], jnp.zeros((D_MODEL, LANES - B_ROPE), w_in.dtype),
                              w_in[:, _B_KR + B_ROPE:]], axis=1).astype(bf)
    wuq = b_w_uq.reshape(B_Q_RANK, B_HEADS, B_NOPE + B_ROPE)
    wuq = jnp.pad(wuq, ((0, 0), (0, 0), (0, 2 * LANES - B_NOPE - B_ROPE))).reshape(B_Q_RANK, -1).astype(bf)
    wukv = b_w_ukv.reshape(B_KV_RANK, B_HEADS, B_NOPE + B_V)
    wuk = wukv[:, :, :B_NOPE].reshape(B_KV_RANK, -1).astype(bf)
    wuv = wukv[:, :, B_NOPE:].reshape(B_KV_RANK, -1).astype(bf)
    lamv = jnp.stack([c_lambda_q1, c_lambda_k1, c_lambda_q2, c_lambda_k2], axis=0)
    table = jnp.concatenate([peer_u, peer_v], axis=1)

    qa, ka, va, qb, kb, vb, qc, kc, vc = _inproj(
        x, w_in_p, tabs, row(a_qnorm_g), row(a_knorm_g), row(b_qnorm_g), row(b_kvnorm_g), wuq, wuk, wuv)
    oa = _attend(qa, ka, va, seqs, kv_heads=2, group=A_HEADS // A_KV_HEADS, dq=HEAD_DIM, out_dtype=bf)
    ob = _attend(qb, kb, vb, seqs, kv_heads=3, group=1, dq=2 * LANES, out_dtype=bf)
    oc = _attend(qc, kc, vc, seqs, kv_heads=2, group=2, dq=LANES, out_dtype=jnp.float32)
    x = _outproj(oa, ob, oc, x, w_o.astype(bf), lamv, row(c_subln_g), row(ln1_g), row(ln1_b), lambda_init)
    idx, gate = _route(x, peer_w_query.astype(bf), peer_sub_keys.astype(bf))
    return _peer(idx, x, gate, row(ln2_g), row(ln2_b), table)


def kernel(x_prompt, x_sample, w_in, a_qnorm_g, a_knorm_g, b_qnorm_g, b_kvnorm_g, b_w_uq, b_w_ukv,
           c_lambda_q1, c_lambda_k1, c_lambda_q2, c_lambda_k2, c_subln_g, w_o, ln1_g, ln1_b,
           peer_w_query, peer_sub_keys, peer_u, peer_v, ln2_g, ln2_b):
    nb_p, s_p, _ = x_prompt.shape
    nb_s, s_s, _ = x_sample.shape
    t_p = nb_p * s_p
    x = jnp.concatenate([x_prompt.reshape(t_p, D_MODEL), x_sample.reshape(nb_s * s_s, D_MODEL)], axis=0)
    seqs = ((0, nb_p, s_p), (t_p, nb_s, s_s))
    tabs = _rope_tables([s_p] * nb_p + [s_s] * nb_s)
    params = (w_in, a_qnorm_g, a_knorm_g, b_qnorm_g, b_kvnorm_g, b_w_uq, b_w_ukv, c_lambda_q1, c_lambda_k1,
              c_lambda_q2, c_lambda_k2, c_subln_g, w_o, ln1_g, ln1_b, peer_w_query, peer_sub_keys, peer_u,
              peer_v, ln2_g, ln2_b)
    for l in range(DEPTH):
        x = _layer(x, seqs, tabs, 0.8 - 0.6 * math.exp(-0.3 * l), *(p[l] for p in params))

    y_prompt = x[:t_p].reshape(nb_p, s_p, D_MODEL)
    y_sample = x[t_p:].reshape(nb_s, s_s, D_MODEL)
    return (y_prompt, y_sample)
```
